```python
import math
import jax, jax.numpy as jnp
from jax import lax
import numpy as np

D_MODEL = 4096
BATCH = 8
SEQ = 2048
DEPTH = 4
DEC_BATCH = 2
DEC_SEQ = 4096
PAST_LEN = 128

W_MIX = D_MODEL
W_A = W_MIX // 2
W_B = W_MIX - W_A
CHUNK = 128
H_A = 16
HD_A = W_A // H_A
HYENA_ORDER = 2
N_DIR = 2
SHORT_K = 3
POS_EMB = 33
FILT_HIDDEN = 64
D_FF = 4 * D_MODEL
N_MOD = 6
EPS = 1e-6
DECAY_SLOW = abs(math.log(1e-2)) / 1.5
DECAY_FAST = abs(math.log(1e-2)) / 0.3

kernel_name = "hybrid_gmlp_hyena_adaln_encoder"


def rms_norm(x, g):
    x32 = x.astype(jnp.float32)
    y = x32 * lax.rsqrt(jnp.mean(x32 * x32, axis=-1, keepdims=True) + EPS)
    return y.astype(x.dtype) * g


def layer_norm(x, g, b):
    x32 = x.astype(jnp.float32)
    mu = jnp.mean(x32, axis=-1, keepdims=True)
    xc = x32 - mu
    y = xc * lax.rsqrt(jnp.mean(xc * xc, axis=-1, keepdims=True) + EPS)
    return y.astype(x.dtype) * g + b


def hyena_pos_features(L):
    t = jnp.linspace(0.0, 1.0, L, dtype=jnp.float32)[:, None]
    bands = (POS_EMB - 1) // 2
    w = 2.0 * math.pi * jnp.arange(L, dtype=jnp.float32) / L
    f = jnp.linspace(1e-4, bands - 1, bands, dtype=jnp.float32)
    ang = w[:, None] * f[None, :]
    z = jnp.concatenate([t, jnp.cos(ang), -jnp.sin(ang)], axis=-1)
    return z, t


def hyena_filter_spectra(z, t, f_w1, f_b1, f_w2, f_b2, f_w3, f_b3, f_freq, f_w_out, log_decay):
    f32 = jnp.float32
    L = z.shape[0]
    freq = f_freq.astype(f32)
    h = jnp.sin(freq[0] * (z @ f_w1.astype(f32) + f_b1.astype(f32)))
    h = jnp.sin(freq[1] * (h @ f_w2.astype(f32) + f_b2.astype(f32)))
    h = jnp.sin(freq[2] * (h @ f_w3.astype(f32) + f_b3.astype(f32)))
    k = h @ f_w_out.astype(f32)
    k = k * jnp.exp(-t * jnp.exp(log_decay.astype(f32)))
    k = k.reshape(L, HYENA_ORDER, N_DIR, W_B)
    fwd = k[:, :, 0]
    bwd = k[:, :, 1]
    kc = jnp.concatenate([fwd, jnp.zeros((1, HYENA_ORDER, W_B), f32), bwd[:0:-1]], axis=0)
    kc = kc * lax.rsqrt(jnp.sum(kc * kc, axis=0, keepdims=True) + EPS)
    return jnp.fft.rfft(kc, axis=0)


def long_conv(u, spec, d):
    L = u.shape[1]
    u32 = u.astype(jnp.float32)
    y = jnp.fft.irfft(jnp.fft.rfft(u32, n=2 * L, axis=1) * spec, n=2 * L, axis=1)[:, :L]
    return (y + u32 * d.astype(jnp.float32)).astype(u.dtype)


def hyena_mixer(z, conv_w, conv_b, spec, hyena_d):
    L = z.shape[1]
    pad = SHORT_K // 2
    zp = jnp.pad(z, ((0, 0), (pad, pad), (0, 0)))
    zc = conv_b + zp[:, 0:L] * conv_w[0] + zp[:, 1:L + 1] * conv_w[1] + zp[:, 2:L + 2] * conv_w[2]
    v, x1, x2 = jnp.split(zc, 3, axis=-1)
    y = x1 * long_conv(v, spec[:, 0], hyena_d[0])
    y = x2 * long_conv(y, spec[:, 1], hyena_d[1])
    return y


def gmlp_mixer(z, ln_g, ln_b, w_s, b_s):
    B, L, _ = z.shape
    z = jax.nn.gelu(z)
    u, v = jnp.split(z, 2, axis=-1)
    v = layer_norm(v, ln_g, ln_b).reshape(B, L // CHUNK, CHUNK, H_A, HD_A)
    v = jnp.einsum('hqk,bnkhd->bnqhd', w_s, v) + b_s.T[:, :, None]
    return u * v.reshape(B, L, W_A)


def encoder_trunk(x, c, w_ada, b_ada, g_norm1, g_norm2, w_in, ln_g_a, ln_b_a, w_s, b_s,
                  conv_w, conv_b, f_w1, f_b1, f_w2, f_b2, f_w3, f_b3, f_freq, f_w_out,
                  log_decay, hyena_d, g_grp_a, g_grp_b, w_out, w_up, w_down, g_final):
    L = x.shape[1]
    zpos, t = hyena_pos_features(L)
    cs = jax.nn.silu(c)
    for l in range(DEPTH):
        mod = (cs @ w_ada[l] + b_ada[l])[:, None, :]
        sh1, sc1, g1, sh2, sc2, g2 = jnp.split(mod, N_MOD, axis=-1)
        h = rms_norm(x, g_norm1[l]) * (1.0 + sc1) + sh1
        zin = h @ w_in[l]
        ya = gmlp_mixer(zin[..., :2 * W_A], ln_g_a[l], ln_b_a[l], w_s[l], b_s[l])
        spec = hyena_filter_spectra(zpos, t, f_w1[l], f_b1[l], f_w2[l], f_b2[l], f_w3[l], f_b3[l],
                                    f_freq[l], f_w_out[l], log_decay[l])
        yb = hyena_mixer(zin[..., 2 * W_A:], conv_w[l], conv_b[l], spec, hyena_d[l])
        y = jnp.concatenate([rms_norm(ya, g_grp_a[l]), rms_norm(yb, g_grp_b[l])], axis=-1)
        x = x + g1 * (y @ w_out[l])
        h = rms_norm(x, g_norm2[l]) * (1.0 + sc2) + sh2
        x = x + g2 * (jnp.square(jax.nn.relu(h @ w_up[l])) @ w_down[l])
    return rms_norm(x, g_final)


def setup_inputs(seed: int = 0) -> dict:
    key = jax.random.key(seed)
    ks = jax.random.split(key, 40)
    f32 = jnp.float32

    def nrm(k, shape, scale):
        return scale * jax.random.normal(k, shape, f32)

    def gain(k, shape):
        return 1.0 + nrm(k, shape, 0.02)

    return {
        "x_prompt": nrm(ks[0], (BATCH, SEQ, D_MODEL), 1.0),
        "x_sample": nrm(ks[1], (DEC_BATCH, DEC_SEQ, D_MODEL), 1.0),
        "c_prompt": nrm(ks[2], (BATCH, D_MODEL), 1.0),
        "c_sample": nrm(ks[3], (DEC_BATCH, D_MODEL), 1.0),
        "w_ada": nrm(ks[4], (DEPTH, D_MODEL, N_MOD * D_MODEL), 0.5 * D_MODEL ** -0.5),
        "b_ada": nrm(ks[5], (DEPTH, N_MOD * D_MODEL), 0.02),
        "g_norm1": gain(ks[6], (DEPTH, D_MODEL)),
        "g_norm2": gain(ks[7], (DEPTH, D_MODEL)),
        "w_in": nrm(ks[8], (DEPTH, D_MODEL, 2 * W_A + 3 * W_B), D_MODEL ** -0.5),
        "ln_g_a": gain(ks[9], (DEPTH, W_A)),
        "ln_b_a": nrm(ks[10], (DEPTH, W_A), 0.02),
        "w_s": nrm(ks[11], (DEPTH, H_A, CHUNK, CHUNK), CHUNK ** -0.5),
        "b_s": gain(ks[12], (DEPTH, H_A, CHUNK)),
        "conv_w": nrm(ks[13], (DEPTH, SHORT_K, 3 * W_B), SHORT_K ** -0.5),
        "conv_b": nrm(ks[14], (DEPTH, 3 * W_B), 0.02),
        "f_w1": nrm(ks[15], (DEPTH, POS_EMB, FILT_HIDDEN), POS_EMB ** -0.5),
        "f_b1": nrm(ks[16], (DEPTH, FILT_HIDDEN), 0.02),
        "f_w2": nrm(ks[17], (DEPTH, FILT_HIDDEN, FILT_HIDDEN), FILT_HIDDEN ** -0.5),
        "f_b2": nrm(ks[18], (DEPTH, FILT_HIDDEN), 0.02),
        "f_w3": nrm(ks[19], (DEPTH, FILT_HIDDEN, FILT_HIDDEN), FILT_HIDDEN ** -0.5),
        "f_b3": nrm(ks[20], (DEPTH, FILT_HIDDEN), 0.02),
        "f_freq": gain(ks[21], (DEPTH, 3, FILT_HIDDEN)),
        "f_w_out": nrm(ks[22], (DEPTH, FILT_HIDDEN, HYENA_ORDER * N_DIR * W_B), FILT_HIDDEN ** -0.5),
        "log_decay": jax.random.uniform(ks[23], (DEPTH, HYENA_ORDER * N_DIR * W_B), f32,
                                        math.log(DECAY_SLOW), math.log(DECAY_FAST)),
        "hyena_d": nrm(ks[24], (DEPTH, HYENA_ORDER, W_B), 0.5),
        "g_grp_a": gain(ks[25], (DEPTH, W_A)),
        "g_grp_b": gain(ks[26], (DEPTH, W_B)),
        "w_out": nrm(ks[27], (DEPTH, W_MIX, D_MODEL), W_MIX ** -0.5),
        "w_up": nrm(ks[28], (DEPTH, D_MODEL, D_FF), D_MODEL ** -0.5),
        "w_down": nrm(ks[29], (DEPTH, D_FF, D_MODEL), D_FF ** -0.5),
        "g_final": gain(ks[30], (D_MODEL,)),
    }


def reference(x_prompt, x_sample, c_prompt, c_sample, w_ada, b_ada, g_norm1, g_norm2, w_in,
              ln_g_a, ln_b_a, w_s, b_s, conv_w, conv_b, f_w1, f_b1, f_w2, f_b2, f_w3, f_b3,
              f_freq, f_w_out, log_decay, hyena_d, g_grp_a, g_grp_b, w_out, w_up, w_down, g_final):
    y_prompt = encoder_trunk(x_prompt, c_prompt, w_ada, b_ada, g_norm1, g_norm2, w_in, ln_g_a, ln_b_a,
                             w_s, b_s, conv_w, conv_b, f_w1, f_b1, f_w2, f_b2, f_w3, f_b3, f_freq,
                             f_w_out, log_decay, hyena_d, g_grp_a, g_grp_b, w_out, w_up, w_down, g_final)
    y_sample = encoder_trunk(x_sample, c_sample, w_ada, b_ada, g_norm1, g_norm2, w_in, ln_g_a, ln_b_a,
                             w_s, b_s, conv_w, conv_b, f_w1, f_b1, f_w2, f_b2, f_w3, f_b3, f_freq,
                             f_w_out, log_decay, hyena_d, g_grp_a, g_grp_b, w_out, w_up, w_down, g_final)
    return (y_prompt, y_sample)
```

```python
import functools
import math

import jax
import jax.numpy as jnp
from jax import lax
from jax.experimental import pallas as pl
from jax.experimental.pallas import tpu as pltpu

F32 = jnp.float32
BF16 = jnp.bfloat16
EPS = 1e-6
N_MOD = 6
MOD_ROWS = 16
V7X_VMEM_LIMIT_BYTES = 56 * 1024 * 1024
MXU_TILE = 1024
FREQ_BLOCK = MXU_TILE // 2
CONV_TILE_N = 512
CONV_TILE_K = 2048
HIGHEST = lax.Precision.HIGHEST


def _tile(dim, pref):
    t = min(dim, pref)
    while dim % t:
        t //= 2
    return t


def _params(*semantics):
    return pltpu.CompilerParams(dimension_semantics=semantics,
                                vmem_limit_bytes=V7X_VMEM_LIMIT_BYTES)


def _dot(a, b):
    return jnp.dot(a, b, preferred_element_type=F32)


def _matmul_body(*refs, n_parts, nk_part, n_extra, n_out, epilogue):
    a_refs = refs[:n_parts]
    b_ref = refs[n_parts]
    extra_refs = refs[n_parts + 1:n_parts + 1 + n_extra]
    out_refs = refs[n_parts + 1 + n_extra:n_parts + 1 + n_extra + n_out]
    nk = n_parts * nk_part
    i, j, k = pl.program_id(0), pl.program_id(1), pl.program_id(2)
    if nk == 1:
        epilogue(_dot(a_refs[0][...], b_ref[...]), i, j, extra_refs, out_refs)
        return
    acc_ref = refs[-1]

    @pl.when(k == 0)
    def _():
        acc_ref[...] = _dot(a_refs[0][...], b_ref[...])

    for p in range(n_parts):
        lo = max(p * nk_part, 1)
        hi = (p + 1) * nk_part
        if hi <= lo:
            continue

        @pl.when((k >= lo) & (k < hi))
        def _(p=p):
            acc_ref[...] += _dot(a_refs[p][...], b_ref[...])

    @pl.when(k == nk - 1)
    def _():
        epilogue(acc_ref[...], i, j, extra_refs, out_refs)


def _matmul(a_parts, b, *, tm, tn, tk, extras, outs, epilogue, name):
    m, kp = a_parts[0].shape
    n = b.shape[1]
    n_parts = len(a_parts)
    assert b.shape[0] == n_parts * kp
    tm, tn, tk = _tile(m, tm), _tile(n, tn), _tile(kp, tk)
    nk_part = kp // tk
    nk = n_parts * nk_part

    def a_map(p):
        return lambda i, j, k: (i, jnp.clip(k - p * nk_part, 0, nk_part - 1))

    in_specs = [pl.BlockSpec((tm, tk), a_map(p)) for p in range(n_parts)]
    in_specs.append(pl.BlockSpec((tk, tn), lambda i, j, k: (k, j)))
    for _, blk, imap in extras:
        in_specs.append(pl.BlockSpec(blk, functools.partial(lambda i, j, k, f: f(i, j), f=imap)))
    out_specs = [pl.BlockSpec(blk, functools.partial(lambda i, j, k, f: f(i, j), f=imap))
                 for _, blk, imap in outs]
    body = functools.partial(_matmul_body, n_parts=n_parts, nk_part=nk_part,
                             n_extra=len(extras), n_out=len(outs), epilogue=epilogue)
    res = pl.pallas_call(
        body,
        grid=(m // tm, n // tn, nk),
        in_specs=in_specs,
        out_specs=out_specs,
        out_shape=[o for o, _, _ in outs],
        scratch_shapes=[pltpu.VMEM((tm, tn), F32)] if nk > 1 else [],
        compiler_params=_params("parallel", "parallel", "arbitrary"),
        name=name,
    )(*a_parts, b, *[e for e, _, _ in extras])
    return res


def _adaln_body(c_ref, w_ref, b_ref, o_ref):
    c = c_ref[...]
    cs = (c / (1.0 + jnp.exp(-c))).astype(BF16)
    o_ref[...] = _dot(cs, w_ref[...].astype(BF16)) + b_ref[...]


def _adaln(c_pad, w_ada, b_ada):
    depth, d, n = w_ada.shape
    tn = _tile(n, 512)
    return pl.pallas_call(
        _adaln_body,
        grid=(depth, n // tn),
        in_specs=[pl.BlockSpec((MOD_ROWS, d), lambda l, j: (0, 0)),
                  pl.BlockSpec((None, d, tn), lambda l, j: (l, 0, j)),
                  pl.BlockSpec((None, 1, tn), lambda l, j: (l, 0, j))],
        out_specs=pl.BlockSpec((None, MOD_ROWS, tn), lambda l, j: (l, 0, j)),
        out_shape=jax.ShapeDtypeStruct((depth, MOD_ROWS, n), F32),
        compiler_params=_params("parallel", "parallel"),
        name="adaln",
    )(c_pad, w_ada, b_ada.reshape(depth, 1, n))


def _norm_body(*refs, has_delta, modulate):
    it = iter(refs)
    x = next(it)[...]
    if has_delta:
        x = x + next(it)[...] * next(it)[...]
    g = next(it)[...]
    if modulate:
        sc = next(it)[...]
        sh = next(it)[...]
    if has_delta:
        next(it)[...] = x
    y = x * lax.rsqrt(jnp.mean(x * x, axis=-1, keepdims=True) + EPS) * g
    if modulate:
        y = y * (1.0 + sc) + sh
    o_ref = next(it)
    o_ref[...] = y.astype(o_ref.dtype)


def _norm(x, gain, *, rows_per_batch, delta=None, gate=None, sc=None, sh=None, out_dtype, name):
    t, d = x.shape
    tr = _tile(rows_per_batch, 128)
    has_delta, modulate = delta is not None, sc is not None

    def row(i):
        return (i * tr) // rows_per_batch

    def mod_spec(chunk):
        return pl.BlockSpec((None, 1, d), lambda i: (row(i), 0, chunk))

    tile = pl.BlockSpec((tr, d), lambda i: (i, 0))
    ins, in_specs = [x], [tile]
    if has_delta:
        ins += [delta, gate[0]]
        in_specs += [tile, mod_spec(gate[1])]
    ins.append(gain.reshape(1, d))
    in_specs.append(pl.BlockSpec((1, d), lambda i: (0, 0)))
    if modulate:
        ins += [sc[0], sh[0]]
        in_specs += [mod_spec(sc[1]), mod_spec(sh[1])]
    out_shape, out_specs = [], []
    if has_delta:
        out_shape.append(jax.ShapeDtypeStruct((t, d), F32))
        out_specs.append(tile)
    out_shape.append(jax.ShapeDtypeStruct((t, d), out_dtype))
    out_specs.append(tile)
    res = pl.pallas_call(
        functools.partial(_norm_body, has_delta=has_delta, modulate=modulate),
        grid=(t // tr,),
        in_specs=in_specs,
        out_specs=out_specs,
        out_shape=out_shape,
        compiler_params=_params("parallel"),
        name=name,
    )(*ins)
    return res if has_delta else res[0]


def _gmlp_body(z_ref, lng_ref, lnb_ref, ws_ref, bs_ref, gg_ref, o_ref, gate_ref, *, chunk, heads, hd):
    wa = heads * hd
    z = z_ref[...]
    z = 0.5 * z * (1.0 + jnp.tanh(math.sqrt(2.0 / math.pi) * (z + 0.044715 * (z * z * z))))
    u = z[:, :wa]
    v = z[:, wa:]
    mu = jnp.mean(v, axis=-1, keepdims=True)
    vc = v - mu
    vn = vc * lax.rsqrt(jnp.mean(vc * vc, axis=-1, keepdims=True) + EPS) * lng_ref[...] + lnb_ref[...]
    vb = vn.astype(BF16)
    for c in range(z.shape[0] // chunk):
        rows = slice(c * chunk, (c + 1) * chunk)
        for h in range(heads):
            cols = slice(h * hd, (h + 1) * hd)
            gate_ref[rows, cols] = _dot(ws_ref[h], vb[rows, cols]) + bs_ref[:, cols]
    ya = u * gate_ref[...]
    yn = ya * lax.rsqrt(jnp.mean(ya * ya, axis=-1, keepdims=True) + EPS) * gg_ref[...]
    o_ref[...] = yn.astype(o_ref.dtype)


def _gmlp(zin, ln_g, ln_b, w_s, b_s, g_grp):
    t = zin.shape[0]
    heads, chunk, _ = w_s.shape
    wa = ln_g.shape[0]
    hd = wa // heads
    tr = _tile(t, 2 * chunk)
    bs_full = jnp.repeat(b_s.T, hd, axis=1)
    vec = pl.BlockSpec((1, wa), lambda i: (0, 0))
    return pl.pallas_call(
        functools.partial(_gmlp_body, chunk=chunk, heads=heads, hd=hd),
        grid=(t // tr,),
        in_specs=[pl.BlockSpec((tr, 2 * wa), lambda i: (i, 0)), vec, vec,
                  pl.BlockSpec((heads, chunk, chunk), lambda i: (0, 0, 0)),
                  pl.BlockSpec((chunk, wa), lambda i: (0, 0)), vec],
        out_specs=pl.BlockSpec((tr, wa), lambda i: (i, 0)),
        out_shape=jax.ShapeDtypeStruct((t, wa), BF16),
        scratch_shapes=[pltpu.VMEM((tr, wa), F32)],
        compiler_params=_params("parallel"),
        name="gmlp",
    )(zin, ln_g.reshape(1, wa), ln_b.reshape(1, wa), w_s.astype(BF16), bs_full, g_grp.reshape(1, wa))


def _shortconv_body(*refs, tiles_per_seq):
    main = refs[0:3]
    prev = refs[3:6]
    nxt = refs[6:9]
    w_ref, b_ref = refs[9], refs[10]
    v32_ref, vbf_ref, x1_ref, x2_ref = refs[11:15]
    li = pl.program_id(0) % tiles_per_seq
    tl, c = main[0].shape
    row = lax.broadcasted_iota(jnp.int32, (tl, c), 0)
    keep_prev = jnp.where(li == 0, 0.0, 1.0)
    keep_next = jnp.where(li == tiles_per_seq - 1, 0.0, 1.0)
    for g, outs in enumerate(((v32_ref, vbf_ref), (x1_ref,), (x2_ref,))):
        z = main[g][...]
        before = jnp.where(row == 0, prev[g][7:8, :] * keep_prev, pltpu.roll(z, 1, 0))
        after = jnp.where(row == tl - 1, nxt[g][0:1, :] * keep_next, pltpu.roll(z, tl - 1, 0))
        cols = slice(g * c, (g + 1) * c)
        zc = b_ref[:, cols] + before * w_ref[0:1, cols] + z * w_ref[1:2, cols] + after * w_ref[2:3, cols]
        for o in outs:
            o[...] = zc.astype(o.dtype)


def _shortconv(zin, conv_w, conv_b, *, batch, seq, col0):
    c = conv_w.shape[1] // 3
    tl = _tile(seq, 256)
    tiles_per_seq = seq // tl
    cb0 = col0 // c
    halo = tl // 8
    n_halo = zin.shape[0] // 8

    def main_spec(g):
        return pl.BlockSpec((tl, c), lambda i: (i, cb0 + g))

    def prev_spec(g):
        return pl.BlockSpec((8, c), lambda i: (jnp.maximum(i * halo - 1, 0), cb0 + g))

    def next_spec(g):
        return pl.BlockSpec((8, c), lambda i: (jnp.minimum((i + 1) * halo, n_halo - 1), cb0 + g))

    out_spec = pl.BlockSpec((tl, c), lambda i: (i % tiles_per_seq, i // tiles_per_seq))
    seq_major = lambda dt: jax.ShapeDtypeStruct((seq, batch * c), dt)
    return pl.pallas_call(
        functools.partial(_shortconv_body, tiles_per_seq=tiles_per_seq),
        grid=(batch * tiles_per_seq,),
        in_specs=[main_spec(g) for g in range(3)] + [prev_spec(g) for g in range(3)]
        + [next_spec(g) for g in range(3)]
        + [pl.BlockSpec((3, 3 * c), lambda i: (0, 0)), pl.BlockSpec((1, 3 * c), lambda i: (0, 0))],
        out_specs=[out_spec] * 4,
        out_shape=[seq_major(F32), seq_major(BF16), seq_major(F32), seq_major(F32)],
        compiler_params=_params("parallel"),
        name="shortconv",
    )(*([zin] * 9), conv_w, conv_b.reshape(1, 3 * c))


def _dft_table_body(o_ref, *, seq, fb, transposed):
    tr, tc = o_ref.shape
    r0 = pl.program_id(0) * tr
    c0 = pl.program_id(1) * tc
    rows = lax.broadcasted_iota(jnp.int32, (tr, tc), 0) + r0
    cols = lax.broadcasted_iota(jnp.int32, (tr, tc), 1) + c0
    spec_idx, pos = (cols, rows) if transposed else (rows, cols)
    log_fb = fb.bit_length() - 1
    blk = lax.shift_right_logical(spec_idx, log_fb + 1)
    p = spec_idx & (2 * fb - 1)
    imag = p >= fb
    f = lax.shift_left(blk, log_fb) + (p & (fb - 1))
    nyquist = imag & (f == 0)
    f_eff = jnp.where(nyquist, seq, f)
    shift = jnp.where(imag & (f != 0), seq // 2, 0)
    q = (f_eff * pos + shift) & (2 * seq - 1)
    o_ref[...] = jnp.cos(q.astype(F32) * (math.pi / seq)).astype(o_ref.dtype)


def _dft_table(seq, fb, transposed):
    shape = (seq, 2 * seq) if transposed else (2 * seq, seq)
    tr, tc = _tile(shape[0], 256), _tile(shape[1], 2048)
    return pl.pallas_call(
        functools.partial(_dft_table_body, seq=seq, fb=fb, transposed=transposed),
        grid=(shape[0] // tr, shape[1] // tc),
        out_specs=pl.BlockSpec((tr, tc), lambda i, j: (i, j)),
        out_shape=jax.ShapeDtypeStruct(shape, BF16),
        compiler_params=_params("parallel", "parallel"),
        name="dft_table_inv" if transposed else "dft_table_fwd",
    )()


def _filter_mlp_body(z_ref, w1_ref, b1_ref, w2_ref, b2_ref, w3_ref, b3_ref, fr_ref, o_ref):
    def hdot(a, b):
        return jnp.dot(a, b, precision=HIGHEST, preferred_element_type=F32)

    h = jnp.sin(fr_ref[0:1, :] * (hdot(z_ref[...], w1_ref[...]) + b1_ref[...]))
    h = jnp.sin(fr_ref[1:2, :] * (hdot(h, w2_ref[...]) + b2_ref[...]))
    o_ref[...] = jnp.sin(fr_ref[2:3, :] * (hdot(h, w3_ref[...]) + b3_ref[...]))


def _filter_mlp(zpos, f_w1, f_b1, f_w2, f_b2, f_w3, f_b3, f_freq):
    depth, pe, hid = f_w1.shape
    seq = zpos.shape[0]
    zpad = jnp.pad(zpos, ((0, 0), (0, hid - pe)))
    w1pad = jnp.pad(f_w1, ((0, 0), (0, hid - pe), (0, 0)))
    mat = pl.BlockSpec((None, hid, hid), lambda l: (l, 0, 0))
    vec = pl.BlockSpec((None, 1, hid), lambda l: (l, 0, 0))
    return pl.pallas_call(
        _filter_mlp_body,
        grid=(depth,),
        in_specs=[pl.BlockSpec((seq, hid), lambda l: (0, 0)), mat, vec, mat, vec, mat, vec,
                  pl.BlockSpec((None, 3, hid), lambda l: (l, 0, 0))],
        out_specs=pl.BlockSpec((None, seq, hid), lambda l: (l, 0, 0)),
        out_shape=jax.ShapeDtypeStruct((depth, seq, hid), F32),
        compiler_params=_params("parallel"),
        name="filter_mlp",
    )(zpad, w1pad, f_b1.reshape(depth, 1, hid), f_w2, f_b2.reshape(depth, 1, hid),
      f_w3, f_b3.reshape(depth, 1, hid), f_freq)


def _filter_taps_body(h_ref, wf_ref, wb_ref, df_ref, db_ref, o_ref):
    seq, tc = o_ref.shape[1:]
    h = h_ref[...]
    n = lax.broadcasted_iota(jnp.int32, (seq, tc), 0)
    t = n.astype(F32) * (1.0 / (seq - 1))

    def taps(w_ref, d_ref):
        k = jnp.dot(h, w_ref[...], precision=HIGHEST, preferred_element_type=F32)
        return k * jnp.exp(-t * jnp.exp(d_ref[...]))

    fwd = taps(wf_ref, df_ref)
    bwd = jnp.where(n == 0, 0.0, taps(wb_ref, db_ref))
    ss = jnp.sum(fwd * fwd + bwd * bwd, axis=0, keepdims=True)
    scale = lax.rsqrt(ss + EPS)
    o_ref[0] = ((fwd + bwd) * scale).astype(o_ref.dtype)
    o_ref[1] = ((fwd - bwd) * scale).astype(o_ref.dtype)


def _filter_taps(h3, f_w_out, log_decay, c):
    depth, seq, hid = h3.shape
    order = f_w_out.shape[2] // (2 * c)
    tc = _tile(c, 128)
    ncb = c // tc
    ld = log_decay.reshape(depth, 1, -1)

    def wspec(direction):
        return pl.BlockSpec((None, hid, tc), lambda l, o, j: (l, 0, (2 * o + direction) * ncb + j))

    def dspec(direction):
        return pl.BlockSpec((None, 1, tc), lambda l, o, j: (l, 0, (2 * o + direction) * ncb + j))

    return pl.pallas_call(
        _filter_taps_body,
        grid=(depth, order, ncb),
        in_specs=[pl.BlockSpec((None, seq, hid), lambda l, o, j: (l, 0, 0)),
                  wspec(0), wspec(1), dspec(0), dspec(1)],
        out_specs=pl.BlockSpec((None, 2, seq, tc), lambda l, o, j: (l, 0, 0, o * ncb + j)),
        out_shape=jax.ShapeDtypeStruct((depth, 2, seq, order * c), BF16),
        compiler_params=_params("parallel", "parallel", "parallel"),
        name="filter_taps",
    )(h3, f_w_out, f_w_out, ld, ld)


def _spectrum_body(t_ref, sa_ref, p_ref, *, fb, inv_len):
    r = pl.program_id(2)
    top = _dot(t_ref[0:fb, :], sa_ref[0]) * inv_len
    p_ref[0:fb, :] = top
    p_ref[fb:2 * fb, :] = _dot(t_ref[fb:2 * fb, :], sa_ref[1]) * inv_len

    @pl.when(r == 0)
    def _():
        nyq = _dot(t_ref[fb:fb + 8, :], sa_ref[0])
        p_ref[0:1, :] = top[0:1, :] * 0.5
        p_ref[fb:fb + 1, :] = nyq[0:1, :] * (0.5 * inv_len)


def _spectrum(table_fwd, taps, fb):
    depth, _, seq, n = taps.shape
    tn = _tile(n, 512)
    return pl.pallas_call(
        functools.partial(_spectrum_body, fb=fb, inv_len=1.0 / seq),
        grid=(depth, n // tn, seq // fb),
        in_specs=[pl.BlockSpec((2 * fb, seq), lambda l, j, r: (r, 0)),
                  pl.BlockSpec((None, 2, seq, tn), lambda l, j, r: (l, 0, 0, j))],
        out_specs=pl.BlockSpec((None, 2 * fb, tn), lambda l, j, r: (l, r, j)),
        out_shape=jax.ShapeDtypeStruct((depth, 2 * seq, n), F32),
        compiler_params=_params("parallel", "parallel", "parallel"),
        name="spectrum",
    )(table_fwd, taps)


def _spectral_mul_epilogue(acc, i, j, extra_refs, out_refs, *, fb):
    (p_ref,), (z_ref,) = extra_refs, out_refs
    xr, xi = acc[0:fb, :], acc[fb:2 * fb, :]
    pt, pb = p_ref[0:fb, :], p_ref[fb:2 * fb, :]
    row0 = (lax.broadcasted_iota(jnp.int32, pt.shape, 0) == 0) & (i == 0)
    cross = jnp.where(row0, 0.0, pb)
    diag = jnp.where(row0, pb, pt)
    z_ref[0:fb, :] = (xr * pt - xi * cross).astype(z_ref.dtype)
    z_ref[fb:2 * fb, :] = (xr * cross + xi * diag).astype(z_ref.dtype)


def _gate_epilogue(acc, i, j, extra_refs, out_refs):
    u_ref, gate_ref, d_ref = extra_refs
    y = gate_ref[...] * (acc + d_ref[...] * u_ref[...])
    for o in out_refs:
        o[...] = y.astype(o.dtype)


def _long_conv(table_fwd, table_inv, spec, u_bf, u32, gate, d, *, fb, c, order_idx, out_dtypes):
    seq, n = u32.shape
    tn = _tile(c, CONV_TILE_N)
    ncb = c // tn
    d = d.reshape(d.shape[0], 1, c)
    z = _matmul(
        [table_fwd], u_bf, tm=2 * fb, tn=tn, tk=CONV_TILE_K,
        extras=[(spec, (2 * fb, tn), lambda i, j: (i, order_idx * ncb + j % ncb))],
        outs=[(jax.ShapeDtypeStruct((2 * seq, n), BF16), (2 * fb, tn), lambda i, j: (i, j))],
        epilogue=functools.partial(_spectral_mul_epilogue, fb=fb), name="dft_fwd")[0]
    tm = _tile(seq, MXU_TILE)
    tile = lambda i, j: (i, j)
    return _matmul(
        [table_inv], z, tm=tm, tn=tn, tk=CONV_TILE_K,
        extras=[(u32, (tm, tn), tile), (gate, (tm, tn), tile),
                (d, (None, 1, tn), lambda i, j: (order_idx, 0, j % ncb))],
        outs=[(jax.ShapeDtypeStruct((seq, n), dt), (tm, tn), tile) for dt in out_dtypes],
        epilogue=_gate_epilogue, name="dft_inv")


def _group_norm_to_tokens(y, g, *, batch, seq):
    c = g.shape[0]
    tl = _tile(seq, 256)
    nt = seq // tl

    def body(y_ref, g_ref, o_ref):
        v = y_ref[...]
        o_ref[...] = (v * lax.rsqrt(jnp.mean(v * v, axis=-1, keepdims=True) + EPS) * g_ref[...]).astype(o_ref.dtype)

    return pl.pallas_call(
        body,
        grid=(batch * nt,),
        in_specs=[pl.BlockSpec((tl, c), lambda i: (i % nt, i // nt)),
                  pl.BlockSpec((1, c), lambda i: (0, 0))],
        out_specs=pl.BlockSpec((tl, c), lambda i: (i, 0)),
        out_shape=jax.ShapeDtypeStruct((batch * seq, c), BF16),
        compiler_params=_params("parallel"),
        name="group_norm_b",
    )(y, g.reshape(1, c))


def _ffn_body(h_ref, wu_ref, wd_ref, o_ref):
    f = pl.program_id(1)

    @pl.when(f == 0)
    def _():
        o_ref[...] = jnp.zeros_like(o_ref)

    hid = jnp.square(jnp.maximum(_dot(h_ref[...], wu_ref[...]), 0.0)).astype(BF16)
    d = o_ref.shape[1]
    tn = _tile(d, MXU_TILE)
    for n in range(d // tn):
        cols = slice(n * tn, (n + 1) * tn)
        o_ref[:, cols] += _dot(hid, wd_ref[:, cols])


def _ffn(h, w_up, w_down):
    t, d = h.shape
    dff = w_up.shape[1]
    tm, tf = _tile(t, 512), _tile(dff, 512)
    return pl.pallas_call(
        _ffn_body,
        grid=(t // tm, dff // tf),
        in_specs=[pl.BlockSpec((tm, d), lambda i, f: (i, 0)),
                  pl.BlockSpec((d, tf), lambda i, f: (0, f)),
                  pl.BlockSpec((tf, d), lambda i, f: (f, 0))],
        out_specs=pl.BlockSpec((tm, d), lambda i, f: (i, 0)),
        out_shape=jax.ShapeDtypeStruct((t, d), F32),
        compiler_params=_params("parallel", "arbitrary"),
        name="ffn",
    )(h, w_up, w_down)


def _store_epilogue(acc, i, j, extra_refs, out_refs):
    out_refs[0][...] = acc.astype(out_refs[0].dtype)


def _residual_epilogue(acc, i, j, extra_refs, out_refs):
    x_ref, g_ref = extra_refs
    out_refs[0][...] = x_ref[...] + g_ref[...] * acc


def _hyena_pos_features(seq, pos_emb):
    t = jnp.linspace(0.0, 1.0, seq, dtype=F32)[:, None]
    bands = (pos_emb - 1) // 2
    w = 2.0 * math.pi * jnp.arange(seq, dtype=F32) / seq
    f = jnp.linspace(1e-4, bands - 1, bands, dtype=F32)
    ang = w[:, None] * f[None, :]
    return jnp.concatenate([t, jnp.cos(ang), -jnp.sin(ang)], axis=-1)


def _trunk(x, mod, p):
    batch, seq, d = x.shape
    depth = mod.shape[0]
    wa = p["ln_g_a"].shape[1]
    c = p["hyena_d"].shape[2]
    assert seq & (seq - 1) == 0, "sequence length must be a power of two"
    fb = min(FREQ_BLOCK, seq // 2)
    t = batch * seq
    x = x.reshape(t, d)
    tm = _tile(seq, MXU_TILE)
    tn = _tile(d, MXU_TILE)
    ndb = d // tn

    def mod_rows(i):
        return (i * tm) // seq

    table_fwd = _dft_table(seq, fb, False)
    table_inv = _dft_table(seq, fb, True)
    zpos = _hyena_pos_features(seq, p["f_w1"].shape[1])
    h3 = _filter_mlp(zpos, p["f_w1"], p["f_b1"], p["f_w2"], p["f_b2"], p["f_w3"], p["f_b3"], p["f_freq"])
    spectra = _spectrum(table_fwd, _filter_taps(h3, p["f_w_out"], p["log_decay"], c), fb)

    h = _norm(x, p["g_norm1"][0], rows_per_batch=seq, sc=(mod[0], 1), sh=(mod[0], 0),
              out_dtype=BF16, name="norm_in")
    for l in range(depth):
        m = mod[l]
        n_in = 2 * wa + 3 * c
        zin = _matmul([h], p["w_in"][l], tm=tm, tn=MXU_TILE, tk=d, extras=[],
                      outs=[(jax.ShapeDtypeStruct((t, n_in), F32), (tm, _tile(n_in, MXU_TILE)), lambda i, j: (i, j))],
                      epilogue=_store_epilogue, name="w_in")[0]
        ya = _gmlp(zin, p["ln_g_a"][l], p["ln_b_a"][l], p["w_s"][l], p["b_s"][l], p["g_grp_a"][l])
        v32, vbf, x1, x2 = _shortconv(zin, p["conv_w"][l], p["conv_b"][l], batch=batch, seq=seq, col0=2 * wa)
        conv = functools.partial(_long_conv, table_fwd, table_inv, spectra[l], fb=fb, c=c, d=p["hyena_d"][l])
        y1, y1bf = conv(vbf, v32, x1, order_idx=0, out_dtypes=(F32, BF16))
        (y2,) = conv(y1bf, y1, x2, order_idx=1, out_dtypes=(F32,))
        yb = _group_norm_to_tokens(y2, p["g_grp_b"][l], batch=batch, seq=seq)
        (x,) = _matmul([ya, yb], p["w_out"][l], tm=tm, tn=tn, tk=2 * MXU_TILE,
                       extras=[(x, (tm, tn), lambda i, j: (i, j)),
                               (m, (None, 1, tn), lambda i, j: (mod_rows(i), 0, 2 * ndb + j))],
                       outs=[(jax.ShapeDtypeStruct((t, d), F32), (tm, tn), lambda i, j: (i, j))],
                       epilogue=_residual_epilogue, name="w_out")
        h = _norm(x, p["g_norm2"][l], rows_per_batch=seq, sc=(m, 4), sh=(m, 3), out_dtype=BF16, name="norm_mid")
        ffn = _ffn(h, p["w_up"][l], p["w_down"][l])
        if l + 1 < depth:
            x, h = _norm(x, p["g_norm1"][l + 1], rows_per_batch=seq, delta=ffn, gate=(m, 5),
                         sc=(mod[l + 1], 1), sh=(mod[l + 1], 0), out_dtype=BF16, name="norm_next")
        else:
            x, out = _norm(x, p["g_final"], rows_per_batch=seq, delta=ffn, gate=(m, 5),
                           out_dtype=F32, name="norm_final")
    return out.reshape(batch, seq, d)


def kernel(x_prompt, x_sample, c_prompt, c_sample, w_ada, b_ada, g_norm1, g_norm2, w_in, ln_g_a, ln_b_a, w_s, b_s, conv_w, conv_b, f_w1, f_b1, f_w2, f_b2, f_w3, f_b3, f_freq, f_w_out, log_decay, hyena_d, g_grp_a, g_grp_b, w_out, w_up, w_down, g_final):
    bp, bs = c_prompt.shape[0], c_sample.shape[0]
    assert bp + bs <= MOD_ROWS
    depth, d, _ = w_ada.shape
    c_all = jnp.concatenate([c_prompt, c_sample, jnp.zeros((MOD_ROWS - bp - bs, d), F32)], axis=0)
    mod = _adaln(c_all, w_ada, b_ada).reshape(depth, MOD_ROWS, 1, N_MOD * d)
    p = dict(g_norm1=g_norm1, g_norm2=g_norm2, w_in=w_in.astype(BF16), ln_g_a=ln_g_a, ln_b_a=ln_b_a,
             w_s=w_s, b_s=b_s, conv_w=conv_w, conv_b=conv_b, f_w1=f_w1, f_b1=f_b1, f_w2=f_w2, f_b2=f_b2,
             f_w3=f_w3, f_b3=f_b3, f_freq=f_freq, f_w_out=f_w_out, log_decay=log_decay, hyena_d=hyena_d,
             g_grp_a=g_grp_a, g_grp_b=g_grp_b, w_out=w_out.astype(BF16), w_up=w_up.astype(BF16),
             w_down=w_down.astype(BF16), g_final=g_final)
    y_prompt = _trunk(x_prompt, mod[:, :bp], p)
    y_sample = _trunk(x_sample, mod[:, bp:bp + bs], p)
    return (y_prompt, y_sample)
```

```python
import functools
import math

import jax
import jax.numpy as jnp
from jax import lax
from jax.experimental import pallas as pl
from jax.experimental.pallas import tpu as pltpu

F32 = jnp.float32
BF16 = jnp.bfloat16
EPS = 1e-6
N_MOD = 6
MOD_ROWS = 16
BF16_SUBLANES = 16
V7X_VMEM_LIMIT_BYTES = 56 * 1024 * 1024
MXU_TILE = 1024
FREQ_BLOCK = MXU_TILE // 2
CONV_TILE_K = 2048
ROW_CHUNK = 128
HIGHEST = lax.Precision.HIGHEST
SINGLE = pl.Buffered(1)


def _tile(dim, pref):
    t = min(dim, pref)
    while dim % t:
        t //= 2
    return t


def _params(*semantics):
    return pltpu.CompilerParams(dimension_semantics=semantics,
                                vmem_limit_bytes=V7X_VMEM_LIMIT_BYTES)


def _dot(a, b):
    return jnp.dot(a, b, preferred_element_type=F32)


def _row_chunks(rows, chunk):
    chunk = _tile(rows, chunk)
    return [slice(r, r + chunk) for r in range(0, rows, chunk)]


def _rms(x):
    return x * lax.rsqrt(jnp.mean(x * x, axis=-1, keepdims=True) + EPS)


def _matmul_body(*refs, nk, n_extra, n_out, has_alias, epilogue):
    a_ref, b_ref = refs[0], refs[1]
    extra_refs = refs[2:2 + n_extra]
    first_out = 2 + n_extra + (1 if has_alias else 0)
    out_refs = refs[first_out:first_out + n_out]
    i, j, k = pl.program_id(0), pl.program_id(1), pl.program_id(2)
    if nk == 1:
        epilogue(lambda rows: _dot(a_ref[rows, :], b_ref[...]), i, j, extra_refs, out_refs)
        return
    acc_ref = refs[-1]

    @pl.when(k == 0)
    def _():
        acc_ref[...] = _dot(a_ref[...], b_ref[...])

    if nk > 2:
        @pl.when((k > 0) & (k < nk - 1))
        def _():
            acc_ref[...] += _dot(a_ref[...], b_ref[...])

    @pl.when(k == nk - 1)
    def _():
        epilogue(lambda rows: acc_ref[rows, :] + _dot(a_ref[rows, :], b_ref[...]), i, j, extra_refs, out_refs)


def _matmul(a, b, *, tm, tn, tk, extras, outs, epilogue, name, layer=None, alias=None):
    m, kdim = a.shape
    n = b.shape[-1]
    assert b.shape[-2] == kdim
    tm, tn, tk = _tile(m, tm), _tile(n, tn), _tile(kdim, tk)
    nk = kdim // tk
    in_specs = [pl.BlockSpec((tm, tk), lambda i, j, k: (i, k))]
    if layer is None:
        in_specs.append(pl.BlockSpec((tk, tn), lambda i, j, k: (k, j)))
    else:
        in_specs.append(pl.BlockSpec((None, tk, tn), lambda i, j, k: (layer, k, j)))

    def lift(imap):
        return lambda i, j, k: imap(i, j)

    in_specs += [pl.BlockSpec(blk, lift(imap)) for _, blk, imap in extras]
    operands = [a, b] + [e for e, _, _ in extras]
    aliases = {}
    if alias is not None:
        aliases = {len(operands): 0}
        in_specs.append(pl.BlockSpec(memory_space=pl.ANY))
        operands.append(alias)
    body = functools.partial(_matmul_body, nk=nk, n_extra=len(extras), n_out=len(outs),
                             has_alias=alias is not None, epilogue=epilogue)
    return pl.pallas_call(
        body,
        grid=(m // tm, n // tn, nk),
        in_specs=in_specs,
        out_specs=[pl.BlockSpec(blk, lift(imap)) for _, blk, imap in outs],
        out_shape=[o for o, _, _ in outs],
        scratch_shapes=[pltpu.VMEM((tm, tn), F32)] if nk > 1 else [],
        input_output_aliases=aliases,
        compiler_params=_params("parallel", "parallel", "arbitrary"),
        name=name,
    )(*operands)


def _store_epilogue(acc_of, i, j, extra_refs, out_refs):
    (o_ref,) = out_refs
    for rows in _row_chunks(o_ref.shape[0], o_ref.shape[0] // 2):
        o_ref[rows, :] = acc_of(rows).astype(o_ref.dtype)


def _residual_epilogue(acc_of, i, j, extra_refs, out_refs):
    x_ref, g_ref = extra_refs
    (o_ref,) = out_refs
    for rows in _row_chunks(o_ref.shape[0], o_ref.shape[0] // 2):
        o_ref[rows, :] = x_ref[rows, :] + g_ref[...] * acc_of(rows)


def _adaln_body(c_ref, w_ref, b_ref, o_ref):
    c = c_ref[...]
    cs = (c / (1.0 + jnp.exp(-c))).astype(BF16)
    o_ref[...] = _dot(cs, w_ref[...].astype(BF16)) + b_ref[...]


def _adaln(c_pad, w_ada, b_ada):
    depth, d, n = w_ada.shape
    tn = _tile(n, 512)
    return pl.pallas_call(
        _adaln_body,
        grid=(depth, n // tn),
        in_specs=[pl.BlockSpec((MOD_ROWS, d), lambda l, j: (0, 0)),
                  pl.BlockSpec((None, d, tn), lambda l, j: (l, 0, j)),
                  pl.BlockSpec((None, 1, tn), lambda l, j: (l, 0, j))],
        out_specs=pl.BlockSpec((None, MOD_ROWS, tn), lambda l, j: (l, 0, j)),
        out_shape=jax.ShapeDtypeStruct((depth, MOD_ROWS, n), F32),
        compiler_params=_params("parallel", "parallel"),
        name="adaln",
    )(c_pad, w_ada, b_ada.reshape(depth, 1, n))


def _norm_in_body(x_ref, g_ref, sc_ref, sh_ref, o_ref):
    o_ref[...] = (_rms(x_ref[...]) * g_ref[...] * (1.0 + sc_ref[...]) + sh_ref[...]).astype(o_ref.dtype)


def _norm_in(x, gain, mod, *, seq):
    t, d = x.shape
    tr = _tile(seq, ROW_CHUNK)
    tile = pl.BlockSpec((tr, d), lambda i: (i, 0))

    def mod_spec(chunk):
        return pl.BlockSpec((None, 1, d), lambda i: ((i * tr) // seq, 0, chunk))

    return pl.pallas_call(
        _norm_in_body,
        grid=(t // tr,),
        in_specs=[tile, pl.BlockSpec((1, d), lambda i: (0, 0)), mod_spec(1), mod_spec(0)],
        out_specs=tile,
        out_shape=jax.ShapeDtypeStruct((t, d), BF16),
        compiler_params=_params("parallel"),
        name="norm_in",
    )(x, gain.reshape(1, d), mod, mod)


def _gmlp_body(z_ref, lng_ref, lnb_ref, ws_ref, bs_ref, gg_ref, o_ref, gate_ref, *, chunk, heads, hd):
    wa = heads * hd
    z = z_ref[...].astype(F32)
    z = 0.5 * z * (1.0 + jnp.tanh(math.sqrt(2.0 / math.pi) * (z + 0.044715 * (z * z * z))))
    u = z[:, :wa]
    v = z[:, wa:]
    vc = v - jnp.mean(v, axis=-1, keepdims=True)
    vn = vc * lax.rsqrt(jnp.mean(vc * vc, axis=-1, keepdims=True) + EPS) * lng_ref[...] + lnb_ref[...]
    vb = vn.astype(BF16)
    for rows in _row_chunks(z.shape[0], chunk):
        for h in range(heads):
            cols = slice(h * hd, (h + 1) * hd)
            gate_ref[rows, cols] = _dot(ws_ref[h], vb[rows, cols]) + bs_ref[:, cols]
    o_ref[...] = (_rms(u * gate_ref[...]) * gg_ref[...]).astype(o_ref.dtype)


def _gmlp(zin, ln_g, ln_b, w_s, b_s, g_grp, *, out_cols):
    t = zin.shape[0]
    heads, chunk, _ = w_s.shape
    wa = ln_g.shape[0]
    hd = wa // heads
    tr = _tile(t, 2 * chunk)
    bs_full = jnp.repeat(b_s.T, hd, axis=1)
    vec = pl.BlockSpec((1, wa), lambda i: (0, 0))
    return pl.pallas_call(
        functools.partial(_gmlp_body, chunk=chunk, heads=heads, hd=hd),
        grid=(t // tr,),
        in_specs=[pl.BlockSpec((tr, 2 * wa), lambda i: (i, 0)), vec, vec,
                  pl.BlockSpec((heads, chunk, chunk), lambda i: (0, 0, 0)),
                  pl.BlockSpec((chunk, wa), lambda i: (0, 0)), vec],
        out_specs=pl.BlockSpec((tr, wa), lambda i: (i, 0)),
        out_shape=jax.ShapeDtypeStruct((t, out_cols), BF16),
        scratch_shapes=[pltpu.VMEM((tr, wa), F32)],
        compiler_params=_params("parallel"),
        name="gmlp",
    )(zin, ln_g.reshape(1, wa), ln_b.reshape(1, wa), w_s.astype(BF16), bs_full, g_grp.reshape(1, wa))


def _shortconv_body(*refs, tiles_per_seq):
    main, prev, nxt = refs[0:3], refs[3:6], refs[6:9]
    w_ref, b_ref = refs[9], refs[10]
    outs = refs[11:14]
    li = pl.program_id(0) % tiles_per_seq
    tl, c = main[0].shape
    row = lax.broadcasted_iota(jnp.int32, (tl, c), 0)
    keep_prev = jnp.where(li == 0, 0.0, 1.0)
    keep_next = jnp.where(li == tiles_per_seq - 1, 0.0, 1.0)
    last = BF16_SUBLANES - 1
    for g in range(3):
        z = main[g][...].astype(F32)
        z_prev = prev[g][last:last + 1, :].astype(F32) * keep_prev
        z_next = nxt[g][0:1, :].astype(F32) * keep_next
        before = jnp.where(row == 0, z_prev, pltpu.roll(z, 1, 0))
        after = jnp.where(row == tl - 1, z_next, pltpu.roll(z, tl - 1, 0))
        cols = slice(g * c, (g + 1) * c)
        zc = b_ref[:, cols] + before * w_ref[0:1, cols] + z * w_ref[1:2, cols] + after * w_ref[2:3, cols]
        outs[g][...] = zc.astype(outs[g].dtype)


def _shortconv(zin, conv_w, conv_b, *, batch, seq, col0):
    c = conv_w.shape[1] // 3
    tl = _tile(seq, 256)
    tiles_per_seq = seq // tl
    cb0 = col0 // c
    halo = tl // BF16_SUBLANES
    n_halo = zin.shape[0] // BF16_SUBLANES

    def main_spec(g):
        return pl.BlockSpec((tl, c), lambda i: (i, cb0 + g))

    def prev_spec(g):
        return pl.BlockSpec((BF16_SUBLANES, c), lambda i: (jnp.maximum(i * halo - 1, 0), cb0 + g))

    def next_spec(g):
        return pl.BlockSpec((BF16_SUBLANES, c), lambda i: (jnp.minimum((i + 1) * halo, n_halo - 1), cb0 + g))

    out_spec = pl.BlockSpec((tl, c), lambda i: (i % tiles_per_seq, i // tiles_per_seq))
    return pl.pallas_call(
        functools.partial(_shortconv_body, tiles_per_seq=tiles_per_seq),
        grid=(batch * tiles_per_seq,),
        in_specs=[main_spec(g) for g in range(3)] + [prev_spec(g) for g in range(3)]
        + [next_spec(g) for g in range(3)]
        + [pl.BlockSpec((3, 3 * c), lambda i: (0, 0)), pl.BlockSpec((1, 3 * c), lambda i: (0, 0))],
        out_specs=[out_spec] * 3,
        out_shape=[jax.ShapeDtypeStruct((seq, batch * c), BF16)] * 3,
        compiler_params=_params("parallel"),
        name="shortconv",
    )(*([zin] * 9), conv_w, conv_b.reshape(1, 3 * c))


def _dft_table_body(o_ref, *, seq, fb, transposed):
    tr, tc = o_ref.shape
    r0 = pl.program_id(0) * tr
    c0 = pl.program_id(1) * tc
    rows = lax.broadcasted_iota(jnp.int32, (tr, tc), 0) + r0
    cols = lax.broadcasted_iota(jnp.int32, (tr, tc), 1) + c0
    spec_idx, pos = (cols, rows) if transposed else (rows, cols)
    log_fb = fb.bit_length() - 1
    blk = lax.shift_right_logical(spec_idx, log_fb + 1)
    p = spec_idx & (2 * fb - 1)
    imag = p >= fb
    f = lax.shift_left(blk, log_fb) + (p & (fb - 1))
    nyquist = imag & (f == 0)
    f_eff = jnp.where(nyquist, seq, f)
    shift = jnp.where(imag & (f != 0), seq // 2, 0)
    q = (f_eff * pos + shift) & (2 * seq - 1)
    o_ref[...] = jnp.cos(q.astype(F32) * (math.pi / seq)).astype(o_ref.dtype)


def _dft_table(seq, fb, transposed):
    shape = (seq, 2 * seq) if transposed else (2 * seq, seq)
    tr, tc = _tile(shape[0], 256), _tile(shape[1], 2048)
    return pl.pallas_call(
        functools.partial(_dft_table_body, seq=seq, fb=fb, transposed=transposed),
        grid=(shape[0] // tr, shape[1] // tc),
        out_specs=pl.BlockSpec((tr, tc), lambda i, j: (i, j)),
        out_shape=jax.ShapeDtypeStruct(shape, BF16),
        compiler_params=_params("parallel", "parallel"),
        name="dft_table_inv" if transposed else "dft_table_fwd",
    )()


def _filter_mlp_body(z_ref, w1_ref, b1_ref, w2_ref, b2_ref, w3_ref, b3_ref, fr_ref, o_ref):
    def hdot(a, b):
        return jnp.dot(a, b, precision=HIGHEST, preferred_element_type=F32)

    h = jnp.sin(fr_ref[0:1, :] * (hdot(z_ref[...], w1_ref[...]) + b1_ref[...]))
    h = jnp.sin(fr_ref[1:2, :] * (hdot(h, w2_ref[...]) + b2_ref[...]))
    o_ref[...] = jnp.sin(fr_ref[2:3, :] * (hdot(h, w3_ref[...]) + b3_ref[...]))


def _filter_mlp(zpos, f_w1, f_b1, f_w2, f_b2, f_w3, f_b3, f_freq):
    depth, pe, hid = f_w1.shape
    seq = zpos.shape[0]
    zpad = jnp.pad(zpos, ((0, 0), (0, hid - pe)))
    w1pad = jnp.pad(f_w1, ((0, 0), (0, hid - pe), (0, 0)))
    mat = pl.BlockSpec((None, hid, hid), lambda l: (l, 0, 0))
    vec = pl.BlockSpec((None, 1, hid), lambda l: (l, 0, 0))
    return pl.pallas_call(
        _filter_mlp_body,
        grid=(depth,),
        in_specs=[pl.BlockSpec((seq, hid), lambda l: (0, 0)), mat, vec, mat, vec, mat, vec,
                  pl.BlockSpec((None, 3, hid), lambda l: (l, 0, 0))],
        out_specs=pl.BlockSpec((None, seq, hid), lambda l: (l, 0, 0)),
        out_shape=jax.ShapeDtypeStruct((depth, seq, hid), F32),
        compiler_params=_params("parallel"),
        name="filter_mlp",
    )(zpad, w1pad, f_b1.reshape(depth, 1, hid), f_w2, f_b2.reshape(depth, 1, hid),
      f_w3, f_b3.reshape(depth, 1, hid), f_freq)


def _filter_taps_body(h_ref, wf_ref, wb_ref, df_ref, db_ref, o_ref):
    seq, tc = o_ref.shape[1:]
    h = h_ref[...]
    n = lax.broadcasted_iota(jnp.int32, (seq, tc), 0)
    t = n.astype(F32) * (1.0 / (seq - 1))

    def taps(w_ref, d_ref):
        k = jnp.dot(h, w_ref[...], precision=HIGHEST, preferred_element_type=F32)
        return k * jnp.exp(-t * jnp.exp(d_ref[...]))

    fwd = taps(wf_ref, df_ref)
    bwd = jnp.where(n == 0, 0.0, taps(wb_ref, db_ref))
    ss = jnp.sum(fwd * fwd + bwd * bwd, axis=0, keepdims=True)
    scale = lax.rsqrt(ss + EPS)
    o_ref[0] = ((fwd + bwd) * scale).astype(o_ref.dtype)
    o_ref[1] = ((fwd - bwd) * scale).astype(o_ref.dtype)


def _filter_taps(h3, f_w_out, log_decay, c):
    depth, seq, hid = h3.shape
    order = f_w_out.shape[2] // (2 * c)
    tc = _tile(c, 128)
    ncb = c // tc
    ld = log_decay.reshape(depth, 1, -1)

    def wspec(direction):
        return pl.BlockSpec((None, hid, tc), lambda l, o, j: (l, 0, (2 * o + direction) * ncb + j))

    def dspec(direction):
        return pl.BlockSpec((None, 1, tc), lambda l, o, j: (l, 0, (2 * o + direction) * ncb + j))

    return pl.pallas_call(
        _filter_taps_body,
        grid=(depth, order, ncb),
        in_specs=[pl.BlockSpec((None, seq, hid), lambda l, o, j: (l, 0, 0)),
                  wspec(0), wspec(1), dspec(0), dspec(1)],
        out_specs=pl.BlockSpec((None, 2, seq, tc), lambda l, o, j: (l, 0, 0, o * ncb + j)),
        out_shape=jax.ShapeDtypeStruct((depth, 2, seq, order * c), BF16),
        compiler_params=_params("parallel", "parallel", "parallel"),
        name="filter_taps",
    )(h3, f_w_out, f_w_out, ld, ld)


def _spectrum_body(t_ref, sa_ref, p_ref, *, fb, inv_len):
    r = pl.program_id(2)
    top = _dot(t_ref[0:fb, :], sa_ref[0]) * inv_len
    p_ref[0:fb, :] = top
    p_ref[fb:2 * fb, :] = _dot(t_ref[fb:2 * fb, :], sa_ref[1]) * inv_len

    @pl.when(r == 0)
    def _():
        nyq = _dot(t_ref[fb:fb + 8, :], sa_ref[0])
        p_ref[0:1, :] = top[0:1, :] * 0.5
        p_ref[fb:fb + 1, :] = nyq[0:1, :] * (0.5 * inv_len)


def _spectrum(table_fwd, taps, fb):
    depth, _, seq, n = taps.shape
    tn = _tile(n, 512)
    return pl.pallas_call(
        functools.partial(_spectrum_body, fb=fb, inv_len=1.0 / seq),
        grid=(depth, n // tn, seq // fb),
        in_specs=[pl.BlockSpec((2 * fb, seq), lambda l, j, r: (r, 0)),
                  pl.BlockSpec((None, 2, seq, tn), lambda l, j, r: (l, 0, 0, j))],
        out_specs=pl.BlockSpec((None, 2 * fb, tn), lambda l, j, r: (l, r, j)),
        out_shape=jax.ShapeDtypeStruct((depth, 2 * seq, n), F32),
        compiler_params=_params("parallel", "parallel", "parallel"),
        name="spectrum",
    )(table_fwd, taps)


def _spectral_mul_epilogue(acc_of, i, j, extra_refs, out_refs, *, fb):
    (p_ref,), (z_ref,) = extra_refs, out_refs
    for re in _row_chunks(fb, fb // 2):
        im = slice(fb + re.start, fb + re.stop)
        xr, xi = acc_of(re), acc_of(im)
        pt, pb = p_ref[re, :], p_ref[im, :]
        cross, diag = pb, pt
        if re.start == 0:
            row0 = (lax.broadcasted_iota(jnp.int32, pt.shape, 0) == 0) & (i == 0)
            cross = jnp.where(row0, 0.0, pb)
            diag = jnp.where(row0, pb, pt)
        z_ref[re, :] = (xr * pt - xi * cross).astype(z_ref.dtype)
        z_ref[im, :] = (xr * cross + xi * diag).astype(z_ref.dtype)


def _gate_epilogue(acc_of, i, j, extra_refs, out_refs):
    u_ref, gate_ref, d_ref = extra_refs
    (o_ref,) = out_refs
    for rows in _row_chunks(o_ref.shape[0], o_ref.shape[0] // 2):
        y = gate_ref[rows, :].astype(F32) * (acc_of(rows) + d_ref[...] * u_ref[rows, :].astype(F32))
        o_ref[rows, :] = y.astype(o_ref.dtype)


def _gate_norm_epilogue(acc_of, i, j, extra_refs, out_refs):
    u_ref, gate_ref, d_ref, g_ref = extra_refs
    (o_ref,) = out_refs
    for rows in _row_chunks(o_ref.shape[0], o_ref.shape[0] // 2):
        y = gate_ref[rows, :].astype(F32) * (acc_of(rows) + d_ref[...] * u_ref[rows, :].astype(F32))
        o_ref[rows, :] = (_rms(y) * g_ref[...]).astype(o_ref.dtype)


def _dft_fwd(table_fwd, spectra, u, *, fb, c, layer, order_idx):
    seq, n = u.shape
    tn = _tile(c, MXU_TILE)
    ncb = c // tn
    return _matmul(
        table_fwd, u, tm=2 * fb, tn=tn, tk=CONV_TILE_K,
        extras=[(spectra, (None, 2 * fb, tn), lambda i, j: (layer, i, order_idx * ncb + j % ncb))],
        outs=[(jax.ShapeDtypeStruct((2 * seq, n), BF16), (2 * fb, tn), lambda i, j: (i, j))],
        epilogue=functools.partial(_spectral_mul_epilogue, fb=fb), name="dft_fwd")[0]


def _ffn_body(*refs, final):
    x_ref, gn_ref, sc_ref, sh_ref, gate_ref, wu_ref, wd_ref, gnext_ref = refs[:8]
    if final:
        (o_ref, h_scr) = refs[8:]
        acc_ref = o_ref
    else:
        scn_ref, shn_ref, acc_ref, o_ref, h_scr = refs[8:]
    f = pl.program_id(1)
    tm, d = x_ref.shape
    row_chunks = _row_chunks(tm, ROW_CHUNK)

    @pl.when(f == 0)
    def _():
        for rows in row_chunks:
            h = _rms(x_ref[rows, :]) * gn_ref[...] * (1.0 + sc_ref[...]) + sh_ref[...]
            h_scr[rows, :] = h.astype(h_scr.dtype)
        acc_ref[...] = jnp.zeros_like(acc_ref)

    hid = jnp.square(jnp.maximum(_dot(h_scr[...], wu_ref[...]), 0.0)).astype(BF16)
    tn = _tile(d, MXU_TILE)
    for n in range(d // tn):
        cols = slice(n * tn, (n + 1) * tn)
        acc_ref[:, cols] += _dot(hid, wd_ref[:, cols])

    @pl.when(f == pl.num_programs(1) - 1)
    def _():
        for rows in row_chunks:
            xn = x_ref[rows, :] + gate_ref[...] * acc_ref[rows, :]
            if final:
                o_ref[rows, :] = _rms(xn) * gnext_ref[...]
            else:
                acc_ref[rows, :] = xn
                o_ref[rows, :] = (_rms(xn) * gnext_ref[...] * (1.0 + scn_ref[...]) + shn_ref[...]).astype(o_ref.dtype)


def _ffn(x, mod, g_norm, w_up, w_down, layer, g_next, mod_next, *, seq):
    t, d = x.shape
    dff = w_up.shape[2]
    tm, tf = _tile(seq, 512), _tile(dff, 512)
    final = mod_next is None

    def mod_spec(chunk):
        return pl.BlockSpec((None, 1, d), lambda i, f: ((i * tm) // seq, 0, chunk))

    vec = pl.BlockSpec((1, d), lambda i, f: (0, 0))
    tile = lambda: pl.BlockSpec((tm, d), lambda i, f: (i, 0), pipeline_mode=SINGLE)
    operands = [x, g_norm.reshape(1, d), mod, mod, mod, w_up, w_down, g_next.reshape(1, d)]
    in_specs = [tile(), vec, mod_spec(4), mod_spec(3), mod_spec(5),
                pl.BlockSpec((None, d, tf), lambda i, f: (layer, 0, f)),
                pl.BlockSpec((None, tf, d), lambda i, f: (layer, f, 0)), vec]
    if final:
        out_shape = [jax.ShapeDtypeStruct((t, d), F32)]
    else:
        operands += [mod_next, mod_next]
        in_specs += [mod_spec(1), mod_spec(0)]
        out_shape = [jax.ShapeDtypeStruct((t, d), F32), jax.ShapeDtypeStruct((t, d), BF16)]
    return pl.pallas_call(
        functools.partial(_ffn_body, final=final),
        grid=(t // tm, dff // tf),
        in_specs=in_specs,
        out_specs=[tile() for _ in out_shape],
        out_shape=out_shape,
        scratch_shapes=[pltpu.VMEM((tm, d), BF16)],
        compiler_params=_params("parallel", "arbitrary"),
        name="ffn_final" if final else "ffn",
    )(*operands)


def _hyena_pos_features(seq, pos_emb):
    t = jnp.linspace(0.0, 1.0, seq, dtype=F32)[:, None]
    bands = (pos_emb - 1) // 2
    w = 2.0 * math.pi * jnp.arange(seq, dtype=F32) / seq
    f = jnp.linspace(1e-4, bands - 1, bands, dtype=F32)
    ang = w[:, None] * f[None, :]
    return jnp.concatenate([t, jnp.cos(ang), -jnp.sin(ang)], axis=-1)


def _trunk(x, mod, p):
    batch, seq, d = x.shape
    depth = mod.shape[0]
    wa = p["ln_g_a"].shape[1]
    c = p["hyena_d"].shape[2]
    assert seq & (seq - 1) == 0, "sequence length must be a power of two"
    assert wa == c and wa + c == p["w_out"].shape[1], "the two head groups must split the mixing width evenly"
    fb = min(FREQ_BLOCK, seq // 2)
    t = batch * seq
    x = x.reshape(t, d)
    tm = _tile(seq, MXU_TILE)
    tn = _tile(d, MXU_TILE)
    ndb = d // tn
    tile = lambda i, j: (i, j)

    table_fwd = _dft_table(seq, fb, False)
    table_inv = _dft_table(seq, fb, True)
    zpos = _hyena_pos_features(seq, p["f_w1"].shape[1])
    h3 = _filter_mlp(zpos, p["f_w1"], p["f_b1"], p["f_w2"], p["f_b2"], p["f_w3"], p["f_b3"], p["f_freq"])
    spectra = _spectrum(table_fwd, _filter_taps(h3, p["f_w_out"], p["log_decay"], c), fb)
    n_in = 2 * wa + 3 * c
    tn_in = _tile(n_in, MXU_TILE)
    tn_c = _tile(c, MXU_TILE)
    ncb = c // tn_c
    tm_n = _tile(seq, MXU_TILE // 2)

    h = _norm_in(x, p["g_norm1"][0], mod[0], seq=seq)
    for l in range(depth):
        m = mod[l]
        zin = _matmul(h, p["w_in"], layer=l, tm=tm, tn=tn_in, tk=d, extras=[],
                      outs=[(jax.ShapeDtypeStruct((t, n_in), BF16), (tm, tn_in), tile)],
                      epilogue=_store_epilogue, name="w_in")[0]
        y = _gmlp(zin, p["ln_g_a"][l], p["ln_b_a"][l], p["w_s"][l], p["b_s"][l], p["g_grp_a"][l], out_cols=wa + c)
        v, x1, x2 = _shortconv(zin, p["conv_w"][l], p["conv_b"][l], batch=batch, seq=seq, col0=2 * wa)
        dl = p["hyena_d"][l].reshape(-1, 1, c)
        z = _dft_fwd(table_fwd, spectra, v, fb=fb, c=c, layer=l, order_idx=0)
        y1 = _matmul(table_inv, z, tm=tm, tn=tn_c, tk=CONV_TILE_K,
                     extras=[(v, (tm, tn_c), tile), (x1, (tm, tn_c), tile),
                             (dl, (None, 1, tn_c), lambda i, j: (0, 0, j % ncb))],
                     outs=[(jax.ShapeDtypeStruct((seq, batch * c), BF16), (tm, tn_c), tile)],
                     epilogue=_gate_epilogue, name="dft_inv")[0]
        z = _dft_fwd(table_fwd, spectra, y1, fb=fb, c=c, layer=l, order_idx=1)
        y = _matmul(table_inv, z, tm=tm_n, tn=c, tk=CONV_TILE_K, alias=y,
                    extras=[(y1, (tm_n, c), tile), (x2, (tm_n, c), tile),
                            (dl, (None, 1, c), lambda i, j: (1, 0, 0)),
                            (p["g_grp_b"][l].reshape(1, c), (1, c), lambda i, j: (0, 0))],
                    outs=[(jax.ShapeDtypeStruct((t, wa + c), BF16), (tm_n, c),
                           lambda i, j: (j * (seq // tm_n) + i, 1))],
                    epilogue=_gate_norm_epilogue, name="dft_inv_norm")[0]
        x = _matmul(y, p["w_out"], layer=l, tm=tm, tn=tn, tk=2 * MXU_TILE,
                    extras=[(x, (tm, tn), tile),
                            (m, (None, 1, tn), lambda i, j: ((i * tm) // seq, 0, 2 * ndb + j))],
                    outs=[(jax.ShapeDtypeStruct((t, d), F32), (tm, tn), tile)],
                    epilogue=_residual_epilogue, name="w_out")[0]
        if l + 1 < depth:
            x, h = _ffn(x, m, p["g_norm2"][l], p["w_up"], p["w_down"], l, p["g_norm1"][l + 1], mod[l + 1], seq=seq)
        else:
            (out,) = _ffn(x, m, p["g_norm2"][l], p["w_up"], p["w_down"], l, p["g_final"], None, seq=seq)
    return out.reshape(batch, seq, d)


def kernel(x_prompt, x_sample, c_prompt, c_sample, w_ada, b_ada, g_norm1, g_norm2, w_in, ln_g_a, ln_b_a, w_s, b_s, conv_w, conv_b, f_w1, f_b1, f_w2, f_b2, f_w3, f_b3, f_freq, f_w_out, log_decay, hyena_d, g_grp_a, g_grp_b, w_out, w_up, w_down, g_final):
    bp, bs = c_prompt.shape[0], c_sample.shape[0]
    assert bp + bs <= MOD_ROWS
    depth, d, _ = w_ada.shape
    c_all = jnp.concatenate([c_prompt, c_sample, jnp.zeros((MOD_ROWS - bp - bs, d), F32)], axis=0)
    mod = _adaln(c_all, w_ada, b_ada).reshape(depth, MOD_ROWS, 1, N_MOD * d)
    p = dict(g_norm1=g_norm1, g_norm2=g_norm2, w_in=w_in.astype(BF16), ln_g_a=ln_g_a, ln_b_a=ln_b_a,
             w_s=w_s, b_s=b_s, conv_w=conv_w, conv_b=conv_b, f_w1=f_w1, f_b1=f_b1, f_w2=f_w2, f_b2=f_b2,
             f_w3=f_w3, f_b3=f_b3, f_freq=f_freq, f_w_out=f_w_out, log_decay=log_decay, hyena_d=hyena_d,
             g_grp_a=g_grp_a, g_grp_b=g_grp_b, w_out=w_out.astype(BF16), w_up=w_up.astype(BF16),
             w_down=w_down.astype(BF16), g_final=g_final)
    y_prompt = _trunk(x_prompt, mod[:, :bp], p)
    y_sample = _trunk(x_sample, mod[:, bp:bp + bs], p)
    return (y_prompt, y_sample)
```

```python
import functools
import math

import jax
import jax.numpy as jnp
from jax import lax
from jax.experimental import pallas as pl
from jax.experimental.pallas import tpu as pltpu

F32 = jnp.float32
BF16 = jnp.bfloat16
EPS = 1e-6
N_MOD = 6
MOD_ROWS = 16
BF16_SUBLANES = 16
V7X_VMEM_LIMIT_BYTES = 56 * 1024 * 1024
MXU_TILE = 1024
FREQ_BLOCK = MXU_TILE // 2
CONV_TILE_K = 2048
PHASES = 4
TAP_PAD = 8
SPECTRAL_ROWS = 128
ROW_CHUNK = 128
HIGHEST = lax.Precision.HIGHEST
SINGLE = pl.Buffered(1)


def _tile(dim, pref):
    t = min(dim, pref)
    while dim % t:
        t //= 2
    return t


def _params(*semantics):
    return pltpu.CompilerParams(dimension_semantics=semantics,
                                vmem_limit_bytes=V7X_VMEM_LIMIT_BYTES)


def _dot(a, b):
    return jnp.dot(a, b, preferred_element_type=F32)


def _row_chunks(rows, chunk):
    chunk = _tile(rows, chunk)
    return [slice(r, r + chunk) for r in range(0, rows, chunk)]


def _rms(x):
    return x * lax.rsqrt(jnp.mean(x * x, axis=-1, keepdims=True) + EPS)


def _matmul_body(*refs, nk, n_extra, n_out, has_alias, epilogue):
    a_ref, b_ref = refs[0], refs[1]
    extra_refs = refs[2:2 + n_extra]
    first_out = 2 + n_extra + (1 if has_alias else 0)
    out_refs = refs[first_out:first_out + n_out]
    i, j, k = pl.program_id(0), pl.program_id(1), pl.program_id(2)
    if nk == 1:
        epilogue(lambda rows: _dot(a_ref[rows, :], b_ref[...]), i, j, extra_refs, out_refs)
        return
    acc_ref = refs[-1]

    @pl.when(k == 0)
    def _():
        acc_ref[...] = _dot(a_ref[...], b_ref[...])

    if nk > 2:
        @pl.when((k > 0) & (k < nk - 1))
        def _():
            acc_ref[...] += _dot(a_ref[...], b_ref[...])

    @pl.when(k == nk - 1)
    def _():
        epilogue(lambda rows: acc_ref[rows, :] + _dot(a_ref[rows, :], b_ref[...]), i, j, extra_refs, out_refs)


def _matmul(a, b, *, tm, tn, tk, extras, outs, epilogue, name, layer=None, col_groups=None, alias=None):
    m, kdim = a.shape
    n = b.shape[-1] * (col_groups or 1)
    assert b.shape[-2] == kdim
    tm, tn, tk = _tile(m, tm), _tile(b.shape[-1], tn), _tile(kdim, tk)
    nk = kdim // tk
    in_specs = [pl.BlockSpec((tm, tk), lambda i, j, k: (i, k))]
    if col_groups is not None:
        per_group = b.shape[-1] // tn
        in_specs.append(pl.BlockSpec((None, tk, tn), lambda i, j, k: (j // per_group, k, j % per_group)))
    elif layer is None:
        in_specs.append(pl.BlockSpec((tk, tn), lambda i, j, k: (k, j)))
    else:
        in_specs.append(pl.BlockSpec((None, tk, tn), lambda i, j, k: (layer, k, j)))

    def lift(imap):
        return lambda i, j, k: imap(i, j)

    in_specs += [pl.BlockSpec(blk, lift(imap)) for _, blk, imap in extras]
    operands = [a, b] + [e for e, _, _ in extras]
    aliases = {}
    if alias is not None:
        aliases = {len(operands): 0}
        in_specs.append(pl.BlockSpec(memory_space=pl.ANY))
        operands.append(alias)
    body = functools.partial(_matmul_body, nk=nk, n_extra=len(extras), n_out=len(outs),
                             has_alias=alias is not None, epilogue=epilogue)
    return pl.pallas_call(
        body,
        grid=(m // tm, n // tn, nk),
        in_specs=in_specs,
        out_specs=[pl.BlockSpec(blk, lift(imap)) for _, blk, imap in outs],
        out_shape=[o for o, _, _ in outs],
        scratch_shapes=[pltpu.VMEM((tm, tn), F32)] if nk > 1 else [],
        input_output_aliases=aliases,
        compiler_params=_params("parallel", "parallel", "arbitrary"),
        name=name,
    )(*operands)


def _store_epilogue(acc_of, i, j, extra_refs, out_refs):
    (o_ref,) = out_refs
    for rows in _row_chunks(o_ref.shape[0], o_ref.shape[0] // 2):
        o_ref[rows, :] = acc_of(rows).astype(o_ref.dtype)


def _residual_epilogue(acc_of, i, j, extra_refs, out_refs):
    x_ref, g_ref = extra_refs
    (o_ref,) = out_refs
    for rows in _row_chunks(o_ref.shape[0], o_ref.shape[0] // 2):
        o_ref[rows, :] = x_ref[rows, :] + g_ref[...] * acc_of(rows)


def _adaln_body(c_ref, w_ref, b_ref, o_ref):
    c = c_ref[...]
    cs = (c / (1.0 + jnp.exp(-c))).astype(BF16)
    o_ref[...] = _dot(cs, w_ref[...].astype(BF16)) + b_ref[...]


def _adaln(c_pad, w_ada, b_ada):
    depth, d, n = w_ada.shape
    tn = _tile(n, 512)
    return pl.pallas_call(
        _adaln_body,
        grid=(depth, n // tn),
        in_specs=[pl.BlockSpec((MOD_ROWS, d), lambda l, j: (0, 0)),
                  pl.BlockSpec((None, d, tn), lambda l, j: (l, 0, j)),
                  pl.BlockSpec((None, 1, tn), lambda l, j: (l, 0, j))],
        out_specs=pl.BlockSpec((None, MOD_ROWS, tn), lambda l, j: (l, 0, j)),
        out_shape=jax.ShapeDtypeStruct((depth, MOD_ROWS, n), F32),
        compiler_params=_params("parallel", "parallel"),
        name="adaln",
    )(c_pad, w_ada, b_ada.reshape(depth, 1, n))


def _norm_in_body(x_ref, g_ref, sc_ref, sh_ref, o_ref):
    o_ref[...] = (_rms(x_ref[...]) * g_ref[...] * (1.0 + sc_ref[...]) + sh_ref[...]).astype(o_ref.dtype)


def _norm_in(x, gain, mod, *, seq):
    t, d = x.shape
    tr = _tile(seq, ROW_CHUNK)
    tile = pl.BlockSpec((tr, d), lambda i: (i, 0))

    def mod_spec(chunk):
        return pl.BlockSpec((None, 1, d), lambda i: ((i * tr) // seq, 0, chunk))

    return pl.pallas_call(
        _norm_in_body,
        grid=(t // tr,),
        in_specs=[tile, pl.BlockSpec((1, d), lambda i: (0, 0)), mod_spec(1), mod_spec(0)],
        out_specs=tile,
        out_shape=jax.ShapeDtypeStruct((t, d), BF16),
        compiler_params=_params("parallel"),
        name="norm_in",
    )(x, gain.reshape(1, d), mod, mod)


def _gmlp_body(z_ref, lng_ref, lnb_ref, ws_ref, bs_ref, gg_ref, o_ref, gate_ref, *, chunk, heads, hd):
    wa = heads * hd
    z = z_ref[...].astype(F32)
    z = 0.5 * z * (1.0 + jnp.tanh(math.sqrt(2.0 / math.pi) * (z + 0.044715 * (z * z * z))))
    u = z[:, :wa]
    v = z[:, wa:]
    vc = v - jnp.mean(v, axis=-1, keepdims=True)
    vn = vc * lax.rsqrt(jnp.mean(vc * vc, axis=-1, keepdims=True) + EPS) * lng_ref[...] + lnb_ref[...]
    vb = vn.astype(BF16)
    for rows in _row_chunks(z.shape[0], chunk):
        for h in range(heads):
            cols = slice(h * hd, (h + 1) * hd)
            gate_ref[rows, cols] = _dot(ws_ref[h], vb[rows, cols]) + bs_ref[:, cols]
    o_ref[...] = (_rms(u * gate_ref[...]) * gg_ref[...]).astype(o_ref.dtype)


def _gmlp(zin, ln_g, ln_b, w_s, b_s, g_grp, *, out_cols):
    t = zin.shape[0]
    heads, chunk, _ = w_s.shape
    wa = ln_g.shape[0]
    hd = wa // heads
    tr = _tile(t, 2 * chunk)
    bs_full = jnp.repeat(b_s.T, hd, axis=1)
    vec = pl.BlockSpec((1, wa), lambda i: (0, 0))
    return pl.pallas_call(
        functools.partial(_gmlp_body, chunk=chunk, heads=heads, hd=hd),
        grid=(t // tr,),
        in_specs=[pl.BlockSpec((tr, 2 * wa), lambda i: (i, 0)), vec, vec,
                  pl.BlockSpec((heads, chunk, chunk), lambda i: (0, 0, 0)),
                  pl.BlockSpec((chunk, wa), lambda i: (0, 0)), vec],
        out_specs=pl.BlockSpec((tr, wa), lambda i: (i, 0)),
        out_shape=jax.ShapeDtypeStruct((t, out_cols), BF16),
        scratch_shapes=[pltpu.VMEM((tr, wa), F32)],
        compiler_params=_params("parallel"),
        name="gmlp",
    )(zin, ln_g.reshape(1, wa), ln_b.reshape(1, wa), w_s.astype(BF16), bs_full, g_grp.reshape(1, wa))


def _shortconv_body(*refs, tiles_per_seq):
    main, prev, nxt = refs[0:3], refs[3:6], refs[6:9]
    w_ref, b_ref = refs[9], refs[10]
    outs = refs[11:14]
    li = pl.program_id(0) % tiles_per_seq
    tl, c = main[0].shape
    row = lax.broadcasted_iota(jnp.int32, (tl, c), 0)
    keep_prev = jnp.where(li == 0, 0.0, 1.0)
    keep_next = jnp.where(li == tiles_per_seq - 1, 0.0, 1.0)
    last = BF16_SUBLANES - 1
    for g in range(3):
        z = main[g][...].astype(F32)
        z_prev = prev[g][last:last + 1, :].astype(F32) * keep_prev
        z_next = nxt[g][0:1, :].astype(F32) * keep_next
        before = jnp.where(row == 0, z_prev, pltpu.roll(z, 1, 0))
        after = jnp.where(row == tl - 1, z_next, pltpu.roll(z, tl - 1, 0))
        cols = slice(g * c, (g + 1) * c)
        zc = b_ref[:, cols] + before * w_ref[0:1, cols] + z * w_ref[1:2, cols] + after * w_ref[2:3, cols]
        outs[g][...] = zc.astype(outs[g].dtype)


def _shortconv(zin, conv_w, conv_b, *, batch, seq, col0):
    c = conv_w.shape[1] // 3
    tl = _tile(seq, 256)
    tiles_per_seq = seq // tl
    cb0 = col0 // c
    halo = tl // BF16_SUBLANES
    n_halo = zin.shape[0] // BF16_SUBLANES

    def main_spec(g):
        return pl.BlockSpec((tl, c), lambda i: (i, cb0 + g))

    def prev_spec(g):
        return pl.BlockSpec((BF16_SUBLANES, c), lambda i: (jnp.maximum(i * halo - 1, 0), cb0 + g))

    def next_spec(g):
        return pl.BlockSpec((BF16_SUBLANES, c), lambda i: (jnp.minimum((i + 1) * halo, n_halo - 1), cb0 + g))

    out_spec = pl.BlockSpec((tl, c), lambda i: (i % tiles_per_seq, i // tiles_per_seq))
    return pl.pallas_call(
        functools.partial(_shortconv_body, tiles_per_seq=tiles_per_seq),
        grid=(batch * tiles_per_seq,),
        in_specs=[main_spec(g) for g in range(3)] + [prev_spec(g) for g in range(3)]
        + [next_spec(g) for g in range(3)]
        + [pl.BlockSpec((3, 3 * c), lambda i: (0, 0)), pl.BlockSpec((1, 3 * c), lambda i: (0, 0))],
        out_specs=[out_spec] * 3,
        out_shape=[jax.ShapeDtypeStruct((seq, batch * c), BF16)] * 3,
        compiler_params=_params("parallel"),
        name="shortconv",
    )(*([zin] * 9), conv_w, conv_b.reshape(1, 3 * c))


def _dft_table_body(o_ref, *, seq, fb, transposed):
    tr, tc = o_ref.shape
    r0 = pl.program_id(0) * tr
    c0 = pl.program_id(1) * tc
    rows = lax.broadcasted_iota(jnp.int32, (tr, tc), 0) + r0
    cols = lax.broadcasted_iota(jnp.int32, (tr, tc), 1) + c0
    spec_idx, pos = (cols, rows) if transposed else (rows, cols)
    log_fb = fb.bit_length() - 1
    blk = lax.shift_right_logical(spec_idx, log_fb + 1)
    p = spec_idx & (2 * fb - 1)
    imag = p >= fb
    f = lax.shift_left(blk, log_fb) + (p & (fb - 1))
    nyquist = imag & (f == 0)
    f_eff = jnp.where(nyquist, seq, f)
    shift = jnp.where(imag & (f != 0), seq // 2, 0)
    q = (f_eff * pos + shift) & (2 * seq - 1)
    o_ref[...] = jnp.cos(q.astype(F32) * (math.pi / seq)).astype(o_ref.dtype)


def _dft_table(seq, fb, transposed):
    shape = (seq, 2 * seq) if transposed else (2 * seq, seq)
    tr, tc = _tile(shape[0], 256), _tile(shape[1], 2048)
    return pl.pallas_call(
        functools.partial(_dft_table_body, seq=seq, fb=fb, transposed=transposed),
        grid=(shape[0] // tr, shape[1] // tc),
        out_specs=pl.BlockSpec((tr, tc), lambda i, j: (i, j)),
        out_shape=jax.ShapeDtypeStruct(shape, BF16),
        compiler_params=_params("parallel", "parallel"),
        name="dft_table_inv" if transposed else "dft_table_fwd",
    )()


def _filter_mlp_body(z_ref, w1_ref, b1_ref, w2_ref, b2_ref, w3_ref, b3_ref, fr_ref, o_ref):
    def hdot(a, b):
        return jnp.dot(a, b, precision=HIGHEST, preferred_element_type=F32)

    h = jnp.sin(fr_ref[0:1, :] * (hdot(z_ref[...], w1_ref[...]) + b1_ref[...]))
    h = jnp.sin(fr_ref[1:2, :] * (hdot(h, w2_ref[...]) + b2_ref[...]))
    o_ref[...] = jnp.sin(fr_ref[2:3, :] * (hdot(h, w3_ref[...]) + b3_ref[...]))


def _filter_mlp(zpos, f_w1, f_b1, f_w2, f_b2, f_w3, f_b3, f_freq):
    depth, pe, hid = f_w1.shape
    seq = zpos.shape[0]
    zpad = jnp.pad(zpos, ((0, 0), (0, hid - pe)))
    w1pad = jnp.pad(f_w1, ((0, 0), (0, hid - pe), (0, 0)))
    mat = pl.BlockSpec((None, hid, hid), lambda l: (l, 0, 0))
    vec = pl.BlockSpec((None, 1, hid), lambda l: (l, 0, 0))
    return pl.pallas_call(
        _filter_mlp_body,
        grid=(depth,),
        in_specs=[pl.BlockSpec((seq, hid), lambda l: (0, 0)), mat, vec, mat, vec, mat, vec,
                  pl.BlockSpec((None, 3, hid), lambda l: (l, 0, 0))],
        out_specs=pl.BlockSpec((None, seq, hid), lambda l: (l, 0, 0)),
        out_shape=jax.ShapeDtypeStruct((depth, seq, hid), F32),
        compiler_params=_params("parallel"),
        name="filter_mlp",
    )(zpad, w1pad, f_b1.reshape(depth, 1, hid), f_w2, f_b2.reshape(depth, 1, hid),
      f_w3, f_b3.reshape(depth, 1, hid), f_freq)


def _filter_taps_body(h_ref, wf_ref, wb_ref, df_ref, db_ref, o_ref, fw_scr, bw_scr):
    seq, tc = h_ref.shape[0], o_ref.shape[-1]
    dec = seq // PHASES
    h = h_ref[...]
    n = lax.broadcasted_iota(jnp.int32, (seq, tc), 0)
    t = n.astype(F32) * (1.0 / (seq - 1))

    def taps(w_ref, d_ref):
        k = jnp.dot(h, w_ref[...], precision=HIGHEST, preferred_element_type=F32)
        return k * jnp.exp(-t * jnp.exp(d_ref[...]))

    fwd = taps(wf_ref, df_ref)
    bwd = jnp.where(n == 0, 0.0, taps(wb_ref, db_ref))
    scale = lax.rsqrt(jnp.sum(fwd * fwd + bwd * bwd, axis=0, keepdims=True) + EPS)
    fwd, bwd = fwd * scale, bwd * scale
    head_row = lax.broadcasted_iota(jnp.int32, (TAP_PAD, tc), 0)
    head = jnp.zeros((TAP_PAD, tc), F32)
    for lag in range(1, PHASES):
        head = jnp.where(head_row == TAP_PAD - lag, bwd[lag:lag + 1, :], head)
    fw_scr[0:TAP_PAD, :] = head
    fw_scr[TAP_PAD:TAP_PAD + seq, :] = fwd
    bw_scr[0:TAP_PAD, :] = jnp.zeros((TAP_PAD, tc), F32)
    bw_scr[TAP_PAD:TAP_PAD + seq, :] = bwd
    bw_scr[TAP_PAD + seq:2 * TAP_PAD + seq, :] = jnp.zeros((TAP_PAD, tc), F32)
    first = lax.broadcasted_iota(jnp.int32, (dec, tc), 0) == 0
    for d in range(1 - PHASES, PHASES):
        pos = fw_scr[pl.ds(TAP_PAD + d, dec, stride=PHASES), :]
        neg = jnp.where(first, 0.0, bw_scr[pl.ds(TAP_PAD - d, dec, stride=PHASES), :])
        o_ref[d + PHASES - 1, 0] = (pos + neg).astype(o_ref.dtype)
        o_ref[d + PHASES - 1, 1] = (pos - neg).astype(o_ref.dtype)


def _filter_taps(h3, f_w_out, log_decay, c):
    depth, seq, hid = h3.shape
    order = f_w_out.shape[2] // (2 * c)
    tc = _tile(c, 128)
    ncb = c // tc
    nd = 2 * PHASES - 1
    ld = log_decay.reshape(depth, 1, -1)

    def wspec(direction):
        return pl.BlockSpec((None, hid, tc), lambda l, o, j: (l, 0, (2 * o + direction) * ncb + j))

    def dspec(direction):
        return pl.BlockSpec((None, 1, tc), lambda l, o, j: (l, 0, (2 * o + direction) * ncb + j))

    return pl.pallas_call(
        _filter_taps_body,
        grid=(depth, order, ncb),
        in_specs=[pl.BlockSpec((None, seq, hid), lambda l, o, j: (l, 0, 0)),
                  wspec(0), wspec(1), dspec(0), dspec(1)],
        out_specs=pl.BlockSpec((None, None, nd, 2, seq // PHASES, tc), lambda l, o, j: (l, o, 0, 0, 0, j)),
        out_shape=jax.ShapeDtypeStruct((depth, order, nd, 2, seq // PHASES, c), BF16),
        scratch_shapes=[pltpu.VMEM((seq + TAP_PAD, tc), F32), pltpu.VMEM((seq + 2 * TAP_PAD, tc), F32)],
        compiler_params=_params("parallel", "parallel", "parallel"),
        name="filter_taps",
    )(h3, f_w_out, f_w_out, ld, ld)


def _spectrum_body(t_ref, sa_ref, p_ref, *, fb, inv_len):
    r = pl.program_id(3)
    top = _dot(t_ref[0:fb, :], sa_ref[0]) * inv_len
    bot = _dot(t_ref[fb:2 * fb, :], sa_ref[1]) * inv_len
    p_ref[0:fb, :] = top.astype(p_ref.dtype)
    p_ref[fb:2 * fb, :] = bot.astype(p_ref.dtype)

    @pl.when(r == 0)
    def _():
        head = slice(0, BF16_SUBLANES)
        nyq = _dot(t_ref[fb:fb + BF16_SUBLANES, :], sa_ref[0]) * (0.5 * inv_len)
        row0 = lax.broadcasted_iota(jnp.int32, nyq.shape, 0) == 0
        p_ref[head, :] = jnp.where(row0, top[head, :] * 0.5, top[head, :]).astype(p_ref.dtype)
        p_ref[fb:fb + BF16_SUBLANES, :] = jnp.where(row0, nyq, bot[head, :]).astype(p_ref.dtype)


def _spectrum(table_fwd, taps, fb):
    depth, order, nd, _, dec, c = taps.shape
    tn = _tile(c, 2 * MXU_TILE)
    return pl.pallas_call(
        functools.partial(_spectrum_body, fb=fb, inv_len=1.0 / dec),
        grid=(depth, order * nd, c // tn, dec // fb),
        in_specs=[pl.BlockSpec((2 * fb, dec), lambda l, od, j, r: (r, 0)),
                  pl.BlockSpec((None, None, None, 2, dec, tn), lambda l, od, j, r: (l, od // nd, od % nd, 0, 0, j))],
        out_specs=pl.BlockSpec((None, None, None, 2 * fb, tn), lambda l, od, j, r: (l, od // nd, od % nd, r, j)),
        out_shape=jax.ShapeDtypeStruct((depth, order, nd, 2 * dec, c), BF16),
        compiler_params=_params("parallel", "parallel", "parallel", "parallel"),
        name="spectrum",
    )(table_fwd, taps)


def _poly_fwd_body(*refs, fb):
    nd = 2 * PHASES - 1
    t_ref = refs[0]
    x_refs = refs[1:1 + PHASES]
    g_refs = refs[1 + PHASES:1 + PHASES + nd]
    o_ref = refs[1 + PHASES + nd]
    i = pl.program_id(0)
    tm = t_ref.shape[0]
    for block0 in range(0, tm, 2 * fb):
        for chunk in _row_chunks(fb, SPECTRAL_ROWS):
            re = slice(block0 + chunk.start, block0 + chunk.stop)
            im = slice(re.start + fb, re.stop + fb)
            ar = [_dot(t_ref[re, :], x[...]) for x in x_refs]
            ai = [_dot(t_ref[im, :], x[...]) for x in x_refs]
            gr = [g[re, :].astype(F32) for g in g_refs]
            gi = [g[im, :].astype(F32) for g in g_refs]
            cross, diag = gi, gr
            if re.start == 0:
                row0 = (lax.broadcasted_iota(jnp.int32, gr[0].shape, 0) == 0) & (i == 0)
                cross = [jnp.where(row0, 0.0, v) for v in gi]
                diag = [jnp.where(row0, vi, vr) for vr, vi in zip(gr, gi)]
            for m in range(PHASES):
                br = bi = None
                for p in range(PHASES):
                    d = m - p + PHASES - 1
                    tr = gr[d] * ar[p] - cross[d] * ai[p]
                    ti = cross[d] * ar[p] + diag[d] * ai[p]
                    br = tr if br is None else br + tr
                    bi = ti if bi is None else bi + ti
                o_ref[m, re, :] = br.astype(o_ref.dtype)
                o_ref[m, im, :] = bi.astype(o_ref.dtype)


def _poly_fwd(table_fwd, spectra, u, *, fb, c, layer, order_idx):
    seq, n = u.shape
    dec = seq // PHASES
    uv = u.reshape(dec, PHASES * n)
    tm = _tile(2 * dec, MXU_TILE)
    tn = _tile(c, 512)
    ncol, ncb, nd = n // tn, c // tn, 2 * PHASES - 1

    def phase_spec(p):
        return pl.BlockSpec((dec, tn), lambda i, j: (0, p * ncol + j))

    def g_spec(d):
        return pl.BlockSpec((None, None, None, tm, tn), lambda i, j: (layer, order_idx, d, i, j % ncb))

    return pl.pallas_call(
        functools.partial(_poly_fwd_body, fb=fb),
        grid=(2 * dec // tm, ncol),
        in_specs=[pl.BlockSpec((tm, dec), lambda i, j: (i, 0))]
        + [phase_spec(p) for p in range(PHASES)] + [g_spec(d) for d in range(nd)],
        out_specs=pl.BlockSpec((PHASES, tm, tn), lambda i, j: (0, i, j)),
        out_shape=jax.ShapeDtypeStruct((PHASES, 2 * dec, n), BF16),
        compiler_params=_params("parallel", "parallel"),
        name="poly_fwd",
    )(table_fwd, *([uv] * PHASES), *([spectra] * nd))


def _gate_epilogue(acc_of, i, j, extra_refs, out_refs):
    u_ref, gate_ref, d_ref = extra_refs
    (o_ref,) = out_refs
    for rows in _row_chunks(o_ref.shape[0], o_ref.shape[0] // 2):
        y = gate_ref[rows, :].astype(F32) * (acc_of(rows) + d_ref[...] * u_ref[rows, :].astype(F32))
        o_ref[rows, :] = y.astype(o_ref.dtype)


def _gate_norm_epilogue(acc_of, i, j, extra_refs, out_refs):
    u_ref, gate_ref, d_ref, g_ref = extra_refs
    (o_ref,) = out_refs
    for rows in _row_chunks(o_ref.shape[0], o_ref.shape[0] // 2):
        y = gate_ref[rows, :].astype(F32) * (acc_of(rows) + d_ref[...] * u_ref[rows, :].astype(F32))
        o_ref[rows, :] = (_rms(y) * g_ref[...]).astype(o_ref.dtype)


def _ffn_body(*refs, final):
    x_ref, gn_ref, sc_ref, sh_ref, gate_ref, wu_ref, wd_ref, gnext_ref = refs[:8]
    if final:
        (o_ref, h_scr) = refs[8:]
        acc_ref = o_ref
    else:
        scn_ref, shn_ref, acc_ref, o_ref, h_scr = refs[8:]
    f = pl.program_id(1)
    tm, d = x_ref.shape
    row_chunks = _row_chunks(tm, ROW_CHUNK)

    @pl.when(f == 0)
    def _():
        for rows in row_chunks:
            h = _rms(x_ref[rows, :]) * gn_ref[...] * (1.0 + sc_ref[...]) + sh_ref[...]
            h_scr[rows, :] = h.astype(h_scr.dtype)
        acc_ref[...] = jnp.zeros_like(acc_ref)

    hid = jnp.square(jnp.maximum(_dot(h_scr[...], wu_ref[...]), 0.0)).astype(BF16)
    tn = _tile(d, MXU_TILE)
    for n in range(d // tn):
        cols = slice(n * tn, (n + 1) * tn)
        acc_ref[:, cols] += _dot(hid, wd_ref[:, cols])

    @pl.when(f == pl.num_programs(1) - 1)
    def _():
        for rows in row_chunks:
            xn = x_ref[rows, :] + gate_ref[...] * acc_ref[rows, :]
            if final:
                o_ref[rows, :] = _rms(xn) * gnext_ref[...]
            else:
                acc_ref[rows, :] = xn
                o_ref[rows, :] = (_rms(xn) * gnext_ref[...] * (1.0 + scn_ref[...]) + shn_ref[...]).astype(o_ref.dtype)


def _ffn(x, mod, g_norm, w_up, w_down, layer, g_next, mod_next, *, seq):
    t, d = x.shape
    dff = w_up.shape[2]
    tm, tf = _tile(seq, 512), _tile(dff, 512)
    final = mod_next is None

    def mod_spec(chunk):
        return pl.BlockSpec((None, 1, d), lambda i, f: ((i * tm) // seq, 0, chunk))

    vec = pl.BlockSpec((1, d), lambda i, f: (0, 0))
    tile = lambda: pl.BlockSpec((tm, d), lambda i, f: (i, 0), pipeline_mode=SINGLE)
    operands = [x, g_norm.reshape(1, d), mod, mod, mod, w_up, w_down, g_next.reshape(1, d)]
    in_specs = [tile(), vec, mod_spec(4), mod_spec(3), mod_spec(5),
                pl.BlockSpec((None, d, tf), lambda i, f: (layer, 0, f)),
                pl.BlockSpec((None, tf, d), lambda i, f: (layer, f, 0)), vec]
    if final:
        out_shape = [jax.ShapeDtypeStruct((t, d), F32)]
    else:
        operands += [mod_next, mod_next]
        in_specs += [mod_spec(1), mod_spec(0)]
        out_shape = [jax.ShapeDtypeStruct((t, d), F32), jax.ShapeDtypeStruct((t, d), BF16)]
    return pl.pallas_call(
        functools.partial(_ffn_body, final=final),
        grid=(t // tm, dff // tf),
        in_specs=in_specs,
        out_specs=[tile() for _ in out_shape],
        out_shape=out_shape,
        scratch_shapes=[pltpu.VMEM((tm, d), BF16)],
        compiler_params=_params("parallel", "arbitrary"),
        name="ffn_final" if final else "ffn",
    )(*operands)


def _hyena_pos_features(seq, pos_emb):
    t = jnp.linspace(0.0, 1.0, seq, dtype=F32)[:, None]
    bands = (pos_emb - 1) // 2
    w = 2.0 * math.pi * jnp.arange(seq, dtype=F32) / seq
    f = jnp.linspace(1e-4, bands - 1, bands, dtype=F32)
    ang = w[:, None] * f[None, :]
    return jnp.concatenate([t, jnp.cos(ang), -jnp.sin(ang)], axis=-1)


def _trunk(x, mod, p):
    batch, seq, d = x.shape
    depth = mod.shape[0]
    wa = p["ln_g_a"].shape[1]
    c = p["hyena_d"].shape[2]
    assert seq & (seq - 1) == 0 and seq >= 2 * PHASES * BF16_SUBLANES, "sequence length must be a power of two"
    assert wa == c and wa + c == p["w_out"].shape[1], "the two head groups must split the mixing width evenly"
    dec = seq // PHASES
    fb = min(FREQ_BLOCK, dec // 2)
    t = batch * seq
    ncols = batch * c
    x = x.reshape(t, d)
    tm = _tile(seq, MXU_TILE)
    tn = _tile(d, MXU_TILE)
    ndb = d // tn
    tile = lambda i, j: (i, j)

    def phases(a):
        return a.reshape(dec, PHASES * ncols)

    table_fwd = _dft_table(dec, fb, False)
    table_inv = _dft_table(dec, fb, True)
    zpos = _hyena_pos_features(seq, p["f_w1"].shape[1])
    h3 = _filter_mlp(zpos, p["f_w1"], p["f_b1"], p["f_w2"], p["f_b2"], p["f_w3"], p["f_b3"], p["f_freq"])
    spectra = _spectrum(table_fwd, _filter_taps(h3, p["f_w_out"], p["log_decay"], c), fb)
    n_in = 2 * wa + 3 * c
    tn_in = _tile(n_in, MXU_TILE)
    tm_i = _tile(dec, MXU_TILE)
    tn_c = _tile(c, MXU_TILE * MXU_TILE // tm_i)
    ncb = c // tn_c
    tm_n = _tile(dec, MXU_TILE // 2)
    groups_per_token_row = (wa + c) // c

    h = _norm_in(x, p["g_norm1"][0], mod[0], seq=seq)
    for l in range(depth):
        m = mod[l]
        zin = _matmul(h, p["w_in"], layer=l, tm=tm, tn=tn_in, tk=d, extras=[],
                      outs=[(jax.ShapeDtypeStruct((t, n_in), BF16), (tm, tn_in), tile)],
                      epilogue=_store_epilogue, name="w_in")[0]
        y = _gmlp(zin, p["ln_g_a"][l], p["ln_b_a"][l], p["w_s"][l], p["b_s"][l], p["g_grp_a"][l], out_cols=wa + c)
        v, x1, x2 = _shortconv(zin, p["conv_w"][l], p["conv_b"][l], batch=batch, seq=seq, col0=2 * wa)
        dl = p["hyena_d"][l].reshape(-1, 1, c)
        z = _poly_fwd(table_fwd, spectra, v, fb=fb, c=c, layer=l, order_idx=0)
        y1 = _matmul(table_inv, z, col_groups=PHASES, tm=tm_i, tn=tn_c, tk=CONV_TILE_K,
                     extras=[(phases(v), (tm_i, tn_c), tile), (phases(x1), (tm_i, tn_c), tile),
                             (dl, (None, 1, tn_c), lambda i, j: (0, 0, j % ncb))],
                     outs=[(jax.ShapeDtypeStruct((dec, PHASES * ncols), BF16), (tm_i, tn_c), tile)],
                     epilogue=_gate_epilogue, name="dft_inv")[0]
        z = _poly_fwd(table_fwd, spectra, y1.reshape(seq, ncols), fb=fb, c=c, layer=l, order_idx=1)
        y = _matmul(table_inv, z, col_groups=PHASES, tm=tm_n, tn=c, tk=CONV_TILE_K,
                    alias=y.reshape(t // PHASES, PHASES * (wa + c)),
                    extras=[(y1, (tm_n, c), tile), (phases(x2), (tm_n, c), tile),
                            (dl, (None, 1, c), lambda i, j: (1, 0, 0)),
                            (p["g_grp_b"][l].reshape(1, c), (1, c), lambda i, j: (0, 0))],
                    outs=[(jax.ShapeDtypeStruct((t // PHASES, PHASES * (wa + c)), BF16), (tm_n, c),
                           lambda i, j: ((j % batch) * (dec // tm_n) + i, (j // batch) * groups_per_token_row + 1))],
                    epilogue=_gate_norm_epilogue, name="dft_inv_norm")[0].reshape(t, wa + c)
        x = _matmul(y, p["w_out"], layer=l, tm=tm, tn=tn, tk=2 * MXU_TILE,
                    extras=[(x, (tm, tn), tile),
                            (m, (None, 1, tn), lambda i, j: ((i * tm) // seq, 0, 2 * ndb + j))],
                    outs=[(jax.ShapeDtypeStruct((t, d), F32), (tm, tn), tile)],
                    epilogue=_residual_epilogue, name="w_out")[0]
        if l + 1 < depth:
            x, h = _ffn(x, m, p["g_norm2"][l], p["w_up"], p["w_down"], l, p["g_norm1"][l + 1], mod[l + 1], seq=seq)
        else:
            (out,) = _ffn(x, m, p["g_norm2"][l], p["w_up"], p["w_down"], l, p["g_final"], None, seq=seq)
    return out.reshape(batch, seq, d)


def kernel(x_prompt, x_sample, c_prompt, c_sample, w_ada, b_ada, g_norm1, g_norm2, w_in, ln_g_a, ln_b_a, w_s, b_s, conv_w, conv_b, f_w1, f_b1, f_w2, f_b2, f_w3, f_b3, f_freq, f_w_out, log_decay, hyena_d, g_grp_a, g_grp_b, w_out, w_up, w_down, g_final):
    bp, bs = c_prompt.shape[0], c_sample.shape[0]
    assert bp + bs <= MOD_ROWS
    depth, d, _ = w_ada.shape
    c_all = jnp.concatenate([c_prompt, c_sample, jnp.zeros((MOD_ROWS - bp - bs, d), F32)], axis=0)
    mod = _adaln(c_all, w_ada, b_ada).reshape(depth, MOD_ROWS, 1, N_MOD * d)
    p = dict(g_norm1=g_norm1, g_norm2=g_norm2, w_in=w_in.astype(BF16), ln_g_a=ln_g_a, ln_b_a=ln_b_a,
             w_s=w_s, b_s=b_s, conv_w=conv_w, conv_b=conv_b, f_w1=f_w1, f_b1=f_b1, f_w2=f_w2, f_b2=f_b2,
             f_w3=f_w3, f_b3=f_b3, f_freq=f_freq, f_w_out=f_w_out, log_decay=log_decay, hyena_d=hyena_d,
             g_grp_a=g_grp_a, g_grp_b=g_grp_b, w_out=w_out.astype(BF16), w_up=w_up.astype(BF16),
             w_down=w_down.astype(BF16), g_final=g_final)
    y_prompt = _trunk(x_prompt, mod[:, :bp], p)
    y_sample = _trunk(x_sample, mod[:, bp:bp + bs], p)
    return (y_prompt, y_sample)
```

```python
import functools
import math

import jax
import jax.numpy as jnp
from jax import lax
from jax.experimental import pallas as pl
from jax.experimental.pallas import tpu as pltpu

F32 = jnp.float32
BF16 = jnp.bfloat16
EPS = 1e-6
N_MOD = 6
MOD_ROWS = 16
BF16_SUBLANES = 16
LANES = 128
V7X_VMEM_LIMIT_BYTES = 56 * 1024 * 1024
MXU_TILE = 1024
FREQ_BLOCK = MXU_TILE // 2
CONV_TILE_K = 2048
PHASES = 4
TAP_PAD = 8
SPECTRAL_ROWS = 128
INTERLEAVE_ROWS = 256
ROW_CHUNK = 128
HIGHEST = lax.Precision.HIGHEST
SINGLE = pl.Buffered(1)


def _tile(dim, pref):
    t = min(dim, pref)
    while dim % t:
        t //= 2
    return t


def _params(*semantics):
    return pltpu.CompilerParams(dimension_semantics=semantics,
                                vmem_limit_bytes=V7X_VMEM_LIMIT_BYTES)


def _dot(a, b):
    return jnp.dot(a, b, preferred_element_type=F32)


def _row_chunks(rows, chunk):
    chunk = _tile(rows, chunk)
    return [slice(r, r + chunk) for r in range(0, rows, chunk)]


def _rms(x):
    return x * lax.rsqrt(jnp.mean(x * x, axis=-1, keepdims=True) + EPS)


def _matmul_body(*refs, nk, n_extra, n_out, has_alias, epilogue):
    a_ref, b_ref = refs[0], refs[1]
    extra_refs = refs[2:2 + n_extra]
    first_out = 2 + n_extra + (1 if has_alias else 0)
    out_refs = refs[first_out:first_out + n_out]
    i, j, k = pl.program_id(0), pl.program_id(1), pl.program_id(2)
    if nk == 1:
        epilogue(lambda rows: _dot(a_ref[rows, :], b_ref[...]), i, j, extra_refs, out_refs)
        return
    acc_ref = refs[-1]

    @pl.when(k == 0)
    def _():
        acc_ref[...] = _dot(a_ref[...], b_ref[...])

    if nk > 2:
        @pl.when((k > 0) & (k < nk - 1))
        def _():
            acc_ref[...] += _dot(a_ref[...], b_ref[...])

    @pl.when(k == nk - 1)
    def _():
        epilogue(lambda rows: acc_ref[rows, :] + _dot(a_ref[rows, :], b_ref[...]), i, j, extra_refs, out_refs)


def _matmul(a, b, *, tm, tn, tk, extras, outs, epilogue, name, layer=None, col_groups=None, alias=None):
    m, kdim = a.shape
    n = b.shape[-1] * (col_groups or 1)
    assert b.shape[-2] == kdim
    tm, tn, tk = _tile(m, tm), _tile(b.shape[-1], tn), _tile(kdim, tk)
    nk = kdim // tk
    in_specs = [pl.BlockSpec((tm, tk), lambda i, j, k: (i, k))]
    if col_groups is not None:
        per_group = b.shape[-1] // tn
        in_specs.append(pl.BlockSpec((None, tk, tn), lambda i, j, k: (j // per_group, k, j % per_group)))
    elif layer is None:
        in_specs.append(pl.BlockSpec((tk, tn), lambda i, j, k: (k, j)))
    else:
        in_specs.append(pl.BlockSpec((None, tk, tn), lambda i, j, k: (layer, k, j)))

    def lift(imap):
        return lambda i, j, k: imap(i, j)

    in_specs += [pl.BlockSpec(blk, lift(imap)) for _, blk, imap in extras]
    operands = [a, b] + [e for e, _, _ in extras]
    aliases = {}
    if alias is not None:
        aliases = {len(operands): 0}
        in_specs.append(pl.BlockSpec(memory_space=pl.ANY))
        operands.append(alias)
    body = functools.partial(_matmul_body, nk=nk, n_extra=len(extras), n_out=len(outs),
                             has_alias=alias is not None, epilogue=epilogue)
    return pl.pallas_call(
        body,
        grid=(m // tm, n // tn, nk),
        in_specs=in_specs,
        out_specs=[pl.BlockSpec(blk, lift(imap)) for _, blk, imap in outs],
        out_shape=[o for o, _, _ in outs],
        scratch_shapes=[pltpu.VMEM((tm, tn), F32)] if nk > 1 else [],
        input_output_aliases=aliases,
        compiler_params=_params("parallel", "parallel", "arbitrary"),
        name=name,
    )(*operands)


def _store_epilogue(acc_of, i, j, extra_refs, out_refs):
    (o_ref,) = out_refs
    for rows in _row_chunks(o_ref.shape[0], o_ref.shape[0] // 2):
        o_ref[rows, :] = acc_of(rows).astype(o_ref.dtype)


def _residual_epilogue(acc_of, i, j, extra_refs, out_refs):
    x_ref, g_ref = extra_refs
    (o_ref,) = out_refs
    for rows in _row_chunks(o_ref.shape[0], o_ref.shape[0] // 2):
        o_ref[rows, :] = x_ref[rows, :] + g_ref[...] * acc_of(rows)


def _adaln_body(c_ref, w_ref, b_ref, o_ref):
    c = c_ref[...]
    cs = (c / (1.0 + jnp.exp(-c))).astype(BF16)
    o_ref[...] = _dot(cs, w_ref[...].astype(BF16)) + b_ref[...]


def _adaln(c_pad, w_ada, b_ada):
    depth, d, n = w_ada.shape
    tn = _tile(n, 512)
    return pl.pallas_call(
        _adaln_body,
        grid=(depth, n // tn),
        in_specs=[pl.BlockSpec((MOD_ROWS, d), lambda l, j: (0, 0)),
                  pl.BlockSpec((None, d, tn), lambda l, j: (l, 0, j)),
                  pl.BlockSpec((None, 1, tn), lambda l, j: (l, 0, j))],
        out_specs=pl.BlockSpec((None, MOD_ROWS, tn), lambda l, j: (l, 0, j)),
        out_shape=jax.ShapeDtypeStruct((depth, MOD_ROWS, n), F32),
        compiler_params=_params("parallel", "parallel"),
        name="adaln",
    )(c_pad, w_ada, b_ada.reshape(depth, 1, n))


def _norm_in_body(x_ref, g_ref, sc_ref, sh_ref, o_ref):
    o_ref[...] = (_rms(x_ref[...]) * g_ref[...] * (1.0 + sc_ref[...]) + sh_ref[...]).astype(o_ref.dtype)


def _norm_in(x, gain, mod, *, seq):
    t, d = x.shape
    tr = _tile(seq, ROW_CHUNK)
    tile = pl.BlockSpec((tr, d), lambda i: (i, 0))

    def mod_spec(chunk):
        return pl.BlockSpec((None, 1, d), lambda i: ((i * tr) // seq, 0, chunk))

    return pl.pallas_call(
        _norm_in_body,
        grid=(t // tr,),
        in_specs=[tile, pl.BlockSpec((1, d), lambda i: (0, 0)), mod_spec(1), mod_spec(0)],
        out_specs=tile,
        out_shape=jax.ShapeDtypeStruct((t, d), BF16),
        compiler_params=_params("parallel"),
        name="norm_in",
    )(x, gain.reshape(1, d), mod, mod)


def _gmlp_body(z_ref, lng_ref, lnb_ref, ws_ref, bs_ref, gg_ref, o_ref, gate_ref, *, chunk, heads, hd):
    wa = heads * hd
    z = z_ref[...].astype(F32)
    z = 0.5 * z * (1.0 + jnp.tanh(math.sqrt(2.0 / math.pi) * (z + 0.044715 * (z * z * z))))
    u = z[:, :wa]
    v = z[:, wa:]
    vc = v - jnp.mean(v, axis=-1, keepdims=True)
    vn = vc * lax.rsqrt(jnp.mean(vc * vc, axis=-1, keepdims=True) + EPS) * lng_ref[...] + lnb_ref[...]
    vb = vn.astype(BF16)
    for rows in _row_chunks(z.shape[0], chunk):
        for h in range(heads):
            cols = slice(h * hd, (h + 1) * hd)
            gate_ref[rows, cols] = _dot(ws_ref[h], vb[rows, cols]) + bs_ref[:, cols]
    o_ref[...] = (_rms(u * gate_ref[...]) * gg_ref[...]).astype(o_ref.dtype)


def _gmlp(zin, ln_g, ln_b, w_s, b_s, g_grp, *, out_cols):
    t = zin.shape[0]
    heads, chunk, _ = w_s.shape
    wa = ln_g.shape[0]
    hd = wa // heads
    tr = _tile(t, 2 * chunk)
    bs_full = jnp.repeat(b_s.T, hd, axis=1)
    vec = pl.BlockSpec((1, wa), lambda i: (0, 0))
    return pl.pallas_call(
        functools.partial(_gmlp_body, chunk=chunk, heads=heads, hd=hd),
        grid=(t // tr,),
        in_specs=[pl.BlockSpec((tr, 2 * wa), lambda i: (i, 0)), vec, vec,
                  pl.BlockSpec((heads, chunk, chunk), lambda i: (0, 0, 0)),
                  pl.BlockSpec((chunk, wa), lambda i: (0, 0)), vec],
        out_specs=pl.BlockSpec((tr, wa), lambda i: (i, 0)),
        out_shape=jax.ShapeDtypeStruct((t, out_cols), BF16),
        scratch_shapes=[pltpu.VMEM((tr, wa), F32)],
        compiler_params=_params("parallel"),
        name="gmlp",
    )(zin, ln_g.reshape(1, wa), ln_b.reshape(1, wa), w_s.astype(BF16), bs_full, g_grp.reshape(1, wa))


def _shortconv_body(*refs, tiles_per_seq):
    main, prev, nxt = refs[0:3], refs[3:6], refs[6:9]
    w_ref, b_ref = refs[9], refs[10]
    outs = refs[11:14]
    zc_scr = refs[14]
    li = pl.program_id(0) % tiles_per_seq
    tl, c = main[0].shape
    row = lax.broadcasted_iota(jnp.int32, (tl, c), 0)
    keep_prev = jnp.where(li == 0, 0.0, 1.0)
    keep_next = jnp.where(li == tiles_per_seq - 1, 0.0, 1.0)
    last = BF16_SUBLANES - 1
    for g in range(3):
        z = main[g][...].astype(F32)
        z_prev = prev[g][last:last + 1, :].astype(F32) * keep_prev
        z_next = nxt[g][0:1, :].astype(F32) * keep_next
        before = jnp.where(row == 0, z_prev, pltpu.roll(z, 1, 0))
        after = jnp.where(row == tl - 1, z_next, pltpu.roll(z, tl - 1, 0))
        cols = slice(g * c, (g + 1) * c)
        zc = b_ref[:, cols] + before * w_ref[0:1, cols] + z * w_ref[1:2, cols] + after * w_ref[2:3, cols]
        for k in range(c // LANES):
            lanes = slice(k * LANES, (k + 1) * LANES)
            zc_scr[k] = zc[:, lanes]
            for p in range(PHASES):
                outs[g][p, :, lanes] = zc_scr[k, pl.ds(p, tl // PHASES, stride=PHASES), :].astype(outs[g].dtype)


def _shortconv(zin, conv_w, conv_b, *, batch, seq, col0):
    c = conv_w.shape[1] // 3
    tl = _tile(seq, 256)
    tiles_per_seq = seq // tl
    cb0 = col0 // c
    halo = tl // BF16_SUBLANES
    n_halo = zin.shape[0] // BF16_SUBLANES

    def main_spec(g):
        return pl.BlockSpec((tl, c), lambda i: (i, cb0 + g))

    def prev_spec(g):
        return pl.BlockSpec((BF16_SUBLANES, c), lambda i: (jnp.maximum(i * halo - 1, 0), cb0 + g))

    def next_spec(g):
        return pl.BlockSpec((BF16_SUBLANES, c), lambda i: (jnp.minimum((i + 1) * halo, n_halo - 1), cb0 + g))

    out_spec = pl.BlockSpec((PHASES, tl // PHASES, c), lambda i: (0, i % tiles_per_seq, i // tiles_per_seq))
    return pl.pallas_call(
        functools.partial(_shortconv_body, tiles_per_seq=tiles_per_seq),
        grid=(batch * tiles_per_seq,),
        in_specs=[main_spec(g) for g in range(3)] + [prev_spec(g) for g in range(3)]
        + [next_spec(g) for g in range(3)]
        + [pl.BlockSpec((3, 3 * c), lambda i: (0, 0)), pl.BlockSpec((1, 3 * c), lambda i: (0, 0))],
        out_specs=[out_spec] * 3,
        out_shape=[jax.ShapeDtypeStruct((PHASES, seq // PHASES, batch * c), BF16)] * 3,
        scratch_shapes=[pltpu.VMEM((c // LANES, tl, LANES), F32)],
        compiler_params=_params("parallel"),
        name="shortconv",
    )(*([zin] * 9), conv_w, conv_b.reshape(1, 3 * c))


def _dft_table_body(o_ref, *, seq, fb, transposed):
    tr, tc = o_ref.shape
    r0 = pl.program_id(0) * tr
    c0 = pl.program_id(1) * tc
    rows = lax.broadcasted_iota(jnp.int32, (tr, tc), 0) + r0
    cols = lax.broadcasted_iota(jnp.int32, (tr, tc), 1) + c0
    spec_idx, pos = (cols, rows) if transposed else (rows, cols)
    log_fb = fb.bit_length() - 1
    blk = lax.shift_right_logical(spec_idx, log_fb + 1)
    p = spec_idx & (2 * fb - 1)
    imag = p >= fb
    f = lax.shift_left(blk, log_fb) + (p & (fb - 1))
    nyquist = imag & (f == 0)
    f_eff = jnp.where(nyquist, seq, f)
    shift = jnp.where(imag & (f != 0), seq // 2, 0)
    q = (f_eff * pos + shift) & (2 * seq - 1)
    o_ref[...] = jnp.cos(q.astype(F32) * (math.pi / seq)).astype(o_ref.dtype)


def _dft_table(seq, fb, transposed):
    shape = (seq, 2 * seq) if transposed else (2 * seq, seq)
    tr, tc = _tile(shape[0], 256), _tile(shape[1], 2048)
    return pl.pallas_call(
        functools.partial(_dft_table_body, seq=seq, fb=fb, transposed=transposed),
        grid=(shape[0] // tr, shape[1] // tc),
        out_specs=pl.BlockSpec((tr, tc), lambda i, j: (i, j)),
        out_shape=jax.ShapeDtypeStruct(shape, BF16),
        compiler_params=_params("parallel", "parallel"),
        name="dft_table_inv" if transposed else "dft_table_fwd",
    )()


def _filter_mlp_body(z_ref, w1_ref, b1_ref, w2_ref, b2_ref, w3_ref, b3_ref, fr_ref, o_ref):
    def hdot(a, b):
        return jnp.dot(a, b, precision=HIGHEST, preferred_element_type=F32)

    h = jnp.sin(fr_ref[0:1, :] * (hdot(z_ref[...], w1_ref[...]) + b1_ref[...]))
    h = jnp.sin(fr_ref[1:2, :] * (hdot(h, w2_ref[...]) + b2_ref[...]))
    o_ref[...] = jnp.sin(fr_ref[2:3, :] * (hdot(h, w3_ref[...]) + b3_ref[...]))


def _filter_mlp(zpos, f_w1, f_b1, f_w2, f_b2, f_w3, f_b3, f_freq):
    depth, pe, hid = f_w1.shape
    seq = zpos.shape[0]
    zpad = jnp.pad(zpos, ((0, 0), (0, hid - pe)))
    w1pad = jnp.pad(f_w1, ((0, 0), (0, hid - pe), (0, 0)))
    mat = pl.BlockSpec((None, hid, hid), lambda l: (l, 0, 0))
    vec = pl.BlockSpec((None, 1, hid), lambda l: (l, 0, 0))
    return pl.pallas_call(
        _filter_mlp_body,
        grid=(depth,),
        in_specs=[pl.BlockSpec((seq, hid), lambda l: (0, 0)), mat, vec, mat, vec, mat, vec,
                  pl.BlockSpec((None, 3, hid), lambda l: (l, 0, 0))],
        out_specs=pl.BlockSpec((None, seq, hid), lambda l: (l, 0, 0)),
        out_shape=jax.ShapeDtypeStruct((depth, seq, hid), F32),
        compiler_params=_params("parallel"),
        name="filter_mlp",
    )(zpad, w1pad, f_b1.reshape(depth, 1, hid), f_w2, f_b2.reshape(depth, 1, hid),
      f_w3, f_b3.reshape(depth, 1, hid), f_freq)


def _filter_taps_body(h_ref, wf_ref, wb_ref, df_ref, db_ref, o_ref, fw_scr, bw_scr):
    seq, tc = h_ref.shape[0], o_ref.shape[-1]
    dec = seq // PHASES
    h = h_ref[...]
    n = lax.broadcasted_iota(jnp.int32, (seq, tc), 0)
    t = n.astype(F32) * (1.0 / (seq - 1))

    def taps(w_ref, d_ref):
        k = jnp.dot(h, w_ref[...], precision=HIGHEST, preferred_element_type=F32)
        return k * jnp.exp(-t * jnp.exp(d_ref[...]))

    fwd = taps(wf_ref, df_ref)
    bwd = jnp.where(n == 0, 0.0, taps(wb_ref, db_ref))
    scale = lax.rsqrt(jnp.sum(fwd * fwd + bwd * bwd, axis=0, keepdims=True) + EPS)
    fwd, bwd = fwd * scale, bwd * scale
    head_row = lax.broadcasted_iota(jnp.int32, (TAP_PAD, tc), 0)
    head = jnp.zeros((TAP_PAD, tc), F32)
    for lag in range(1, PHASES):
        head = jnp.where(head_row == TAP_PAD - lag, bwd[lag:lag + 1, :], head)
    fw_scr[0:TAP_PAD, :] = head
    fw_scr[TAP_PAD:TAP_PAD + seq, :] = fwd
    bw_scr[0:TAP_PAD, :] = jnp.zeros((TAP_PAD, tc), F32)
    bw_scr[TAP_PAD:TAP_PAD + seq, :] = bwd
    bw_scr[TAP_PAD + seq:2 * TAP_PAD + seq, :] = jnp.zeros((TAP_PAD, tc), F32)
    first = lax.broadcasted_iota(jnp.int32, (dec, tc), 0) == 0
    for d in range(1 - PHASES, PHASES):
        pos = fw_scr[pl.ds(TAP_PAD + d, dec, stride=PHASES), :]
        neg = jnp.where(first, 0.0, bw_scr[pl.ds(TAP_PAD - d, dec, stride=PHASES), :])
        o_ref[d + PHASES - 1, 0] = (pos + neg).astype(o_ref.dtype)
        o_ref[d + PHASES - 1, 1] = (pos - neg).astype(o_ref.dtype)


def _filter_taps(h3, f_w_out, log_decay, c):
    depth, seq, hid = h3.shape
    order = f_w_out.shape[2] // (2 * c)
    tc = _tile(c, 128)
    ncb = c // tc
    nd = 2 * PHASES - 1
    ld = log_decay.reshape(depth, 1, -1)

    def wspec(direction):
        return pl.BlockSpec((None, hid, tc), lambda l, o, j: (l, 0, (2 * o + direction) * ncb + j))

    def dspec(direction):
        return pl.BlockSpec((None, 1, tc), lambda l, o, j: (l, 0, (2 * o + direction) * ncb + j))

    return pl.pallas_call(
        _filter_taps_body,
        grid=(depth, order, ncb),
        in_specs=[pl.BlockSpec((None, seq, hid), lambda l, o, j: (l, 0, 0)),
                  wspec(0), wspec(1), dspec(0), dspec(1)],
        out_specs=pl.BlockSpec((None, None, nd, 2, seq // PHASES, tc), lambda l, o, j: (l, o, 0, 0, 0, j)),
        out_shape=jax.ShapeDtypeStruct((depth, order, nd, 2, seq // PHASES, c), BF16),
        scratch_shapes=[pltpu.VMEM((seq + TAP_PAD, tc), F32), pltpu.VMEM((seq + 2 * TAP_PAD, tc), F32)],
        compiler_params=_params("parallel", "parallel", "parallel"),
        name="filter_taps",
    )(h3, f_w_out, f_w_out, ld, ld)


def _spectrum_body(t_ref, sa_ref, p_ref, *, fb, inv_len):
    r = pl.program_id(3)
    top = _dot(t_ref[0:fb, :], sa_ref[0]) * inv_len
    bot = _dot(t_ref[fb:2 * fb, :], sa_ref[1]) * inv_len
    p_ref[0:fb, :] = top.astype(p_ref.dtype)
    p_ref[fb:2 * fb, :] = bot.astype(p_ref.dtype)

    @pl.when(r == 0)
    def _():
        head = slice(0, BF16_SUBLANES)
        nyq = _dot(t_ref[fb:fb + BF16_SUBLANES, :], sa_ref[0]) * (0.5 * inv_len)
        row0 = lax.broadcasted_iota(jnp.int32, nyq.shape, 0) == 0
        p_ref[head, :] = jnp.where(row0, top[head, :] * 0.5, top[head, :]).astype(p_ref.dtype)
        p_ref[fb:fb + BF16_SUBLANES, :] = jnp.where(row0, nyq, bot[head, :]).astype(p_ref.dtype)


def _spectrum(table_fwd, taps, fb):
    depth, order, nd, _, dec, c = taps.shape
    tn = _tile(c, 2 * MXU_TILE)
    return pl.pallas_call(
        functools.partial(_spectrum_body, fb=fb, inv_len=1.0 / dec),
        grid=(depth, order * nd, c // tn, dec // fb),
        in_specs=[pl.BlockSpec((2 * fb, dec), lambda l, od, j, r: (r, 0)),
                  pl.BlockSpec((None, None, None, 2, dec, tn), lambda l, od, j, r: (l, od // nd, od % nd, 0, 0, j))],
        out_specs=pl.BlockSpec((None, None, None, 2 * fb, tn), lambda l, od, j, r: (l, od // nd, od % nd, r, j)),
        out_shape=jax.ShapeDtypeStruct((depth, order, nd, 2 * dec, c), BF16),
        compiler_params=_params("parallel", "parallel", "parallel", "parallel"),
        name="spectrum",
    )(table_fwd, taps)


def _poly_fwd_body(*refs, fb):
    nd = 2 * PHASES - 1
    t_ref = refs[0]
    x_refs = refs[1:1 + PHASES]
    g_refs = refs[1 + PHASES:1 + PHASES + nd]
    o_ref = refs[1 + PHASES + nd]
    i = pl.program_id(0)
    tm = t_ref.shape[0]
    for block0 in range(0, tm, 2 * fb):
        for chunk in _row_chunks(fb, SPECTRAL_ROWS):
            re = slice(block0 + chunk.start, block0 + chunk.stop)
            im = slice(re.start + fb, re.stop + fb)
            ar = [_dot(t_ref[re, :], x[...]) for x in x_refs]
            ai = [_dot(t_ref[im, :], x[...]) for x in x_refs]
            gr = [g[re, :].astype(F32) for g in g_refs]
            gi = [g[im, :].astype(F32) for g in g_refs]
            cross, diag = gi, gr
            if re.start == 0:
                row0 = (lax.broadcasted_iota(jnp.int32, gr[0].shape, 0) == 0) & (i == 0)
                cross = [jnp.where(row0, 0.0, v) for v in gi]
                diag = [jnp.where(row0, vi, vr) for vr, vi in zip(gr, gi)]
            for m in range(PHASES):
                br = bi = None
                for p in range(PHASES):
                    d = m - p + PHASES - 1
                    tr = gr[d] * ar[p] - cross[d] * ai[p]
                    ti = cross[d] * ar[p] + diag[d] * ai[p]
                    br = tr if br is None else br + tr
                    bi = ti if bi is None else bi + ti
                o_ref[m, re, :] = br.astype(o_ref.dtype)
                o_ref[m, im, :] = bi.astype(o_ref.dtype)


def _poly_fwd(table_fwd, spectra, u, *, fb, c, layer, order_idx):
    _, dec, n = u.shape
    tm = _tile(2 * dec, MXU_TILE)
    tn = _tile(c, 512)
    ncol, ncb, nd = n // tn, c // tn, 2 * PHASES - 1

    def phase_spec(p):
        return pl.BlockSpec((None, dec, tn), lambda i, j: (p, 0, j))

    def g_spec(d):
        return pl.BlockSpec((None, None, None, tm, tn), lambda i, j: (layer, order_idx, d, i, j % ncb))

    return pl.pallas_call(
        functools.partial(_poly_fwd_body, fb=fb),
        grid=(2 * dec // tm, ncol),
        in_specs=[pl.BlockSpec((tm, dec), lambda i, j: (i, 0))]
        + [phase_spec(p) for p in range(PHASES)] + [g_spec(d) for d in range(nd)],
        out_specs=pl.BlockSpec((PHASES, tm, tn), lambda i, j: (0, i, j)),
        out_shape=jax.ShapeDtypeStruct((PHASES, 2 * dec, n), BF16),
        compiler_params=_params("parallel", "parallel"),
        name="poly_fwd",
    )(table_fwd, *([u] * PHASES), *([spectra] * nd))


def _poly_inv_norm_body(t_ref, z_ref, u_ref, gate_ref, d_ref, g_ref, y_alias_ref, o_ref, rows_scr):
    p = pl.program_id(2)
    tm = t_ref.shape[0]
    y = gate_ref[...].astype(F32) * (_dot(t_ref[...], z_ref[...]) + d_ref[...] * u_ref[...].astype(F32))
    yn = _rms(y) * g_ref[...]
    slabs = [slice(k * LANES, (k + 1) * LANES) for k in range(yn.shape[1] // LANES)]
    for q in range(PHASES):
        @pl.when(p == q)
        def _(q=q):
            for k, lanes in enumerate(slabs):
                rows_scr[k, pl.ds(q, tm, stride=PHASES), :] = yn[:, lanes]

    @pl.when(p == PHASES - 1)
    def _():
        for k, lanes in enumerate(slabs):
            o_ref[:, lanes] = rows_scr[k].astype(o_ref.dtype)


def _poly_inv_norm(table_inv, z, u, gate, d, g, y, *, batch, order_idx):
    dec = table_inv.shape[0]
    c = g.shape[-1]
    tm = _tile(dec, INTERLEAVE_ROWS)
    tiles = dec // tm
    phase_tile = pl.BlockSpec((None, tm, c), lambda i, b, p: (p, i, b))
    return pl.pallas_call(
        _poly_inv_norm_body,
        grid=(tiles, batch, PHASES),
        in_specs=[pl.BlockSpec((tm, 2 * dec), lambda i, b, p: (i, 0)),
                  pl.BlockSpec((None, 2 * dec, c), lambda i, b, p: (p, 0, b)),
                  phase_tile, phase_tile,
                  pl.BlockSpec((None, 1, c), lambda i, b, p: (order_idx, 0, 0)),
                  pl.BlockSpec((1, c), lambda i, b, p: (0, 0)),
                  pl.BlockSpec(memory_space=pl.ANY)],
        out_specs=pl.BlockSpec((PHASES * tm, c), lambda i, b, p: (b * tiles + i, y.shape[1] // c - 1)),
        out_shape=jax.ShapeDtypeStruct(y.shape, y.dtype),
        scratch_shapes=[pltpu.VMEM((c // LANES, PHASES * tm, LANES), F32)],
        input_output_aliases={6: 0},
        compiler_params=_params("parallel", "parallel", "arbitrary"),
        name="dft_inv_norm",
    )(table_inv, z, u, gate, d, g.reshape(1, c), y)


def _gate_epilogue(acc_of, i, j, extra_refs, out_refs):
    u_ref, gate_ref, d_ref = extra_refs
    (o_ref,) = out_refs
    for rows in _row_chunks(o_ref.shape[0], o_ref.shape[0] // 2):
        y = gate_ref[rows, :].astype(F32) * (acc_of(rows) + d_ref[...] * u_ref[rows, :].astype(F32))
        o_ref[rows, :] = y.astype(o_ref.dtype)


def _ffn_body(*refs, final):
    x_ref, gn_ref, sc_ref, sh_ref, gate_ref, wu_ref, wd_ref, gnext_ref = refs[:8]
    if final:
        (o_ref, h_scr) = refs[8:]
        acc_ref = o_ref
    else:
        scn_ref, shn_ref, acc_ref, o_ref, h_scr = refs[8:]
    f = pl.program_id(1)
    tm, d = x_ref.shape
    row_chunks = _row_chunks(tm, ROW_CHUNK)

    @pl.when(f == 0)
    def _():
        for rows in row_chunks:
            h = _rms(x_ref[rows, :]) * gn_ref[...] * (1.0 + sc_ref[...]) + sh_ref[...]
            h_scr[rows, :] = h.astype(h_scr.dtype)
        acc_ref[...] = jnp.zeros_like(acc_ref)

    hid = jnp.square(jnp.maximum(_dot(h_scr[...], wu_ref[...]), 0.0)).astype(BF16)
    tn = _tile(d, MXU_TILE)
    for n in range(d // tn):
        cols = slice(n * tn, (n + 1) * tn)
        acc_ref[:, cols] += _dot(hid, wd_ref[:, cols])

    @pl.when(f == pl.num_programs(1) - 1)
    def _():
        for rows in row_chunks:
            xn = x_ref[rows, :] + gate_ref[...] * acc_ref[rows, :]
            if final:
                o_ref[rows, :] = _rms(xn) * gnext_ref[...]
            else:
                acc_ref[rows, :] = xn
                o_ref[rows, :] = (_rms(xn) * gnext_ref[...] * (1.0 + scn_ref[...]) + shn_ref[...]).astype(o_ref.dtype)


def _ffn(x, mod, g_norm, w_up, w_down, layer, g_next, mod_next, *, seq):
    t, d = x.shape
    dff = w_up.shape[2]
    tm, tf = _tile(seq, 512), _tile(dff, 512)
    final = mod_next is None

    def mod_spec(chunk):
        return pl.BlockSpec((None, 1, d), lambda i, f: ((i * tm) // seq, 0, chunk))

    vec = pl.BlockSpec((1, d), lambda i, f: (0, 0))
    tile = lambda: pl.BlockSpec((tm, d), lambda i, f: (i, 0), pipeline_mode=SINGLE)
    operands = [x, g_norm.reshape(1, d), mod, mod, mod, w_up, w_down, g_next.reshape(1, d)]
    in_specs = [tile(), vec, mod_spec(4), mod_spec(3), mod_spec(5),
                pl.BlockSpec((None, d, tf), lambda i, f: (layer, 0, f)),
                pl.BlockSpec((None, tf, d), lambda i, f: (layer, f, 0)), vec]
    if final:
        out_shape = [jax.ShapeDtypeStruct((t, d), F32)]
    else:
        operands += [mod_next, mod_next]
        in_specs += [mod_spec(1), mod_spec(0)]
        out_shape = [jax.ShapeDtypeStruct((t, d), F32), jax.ShapeDtypeStruct((t, d), BF16)]
    return pl.pallas_call(
        functools.partial(_ffn_body, final=final),
        grid=(t // tm, dff // tf),
        in_specs=in_specs,
        out_specs=[tile() for _ in out_shape],
        out_shape=out_shape,
        scratch_shapes=[pltpu.VMEM((tm, d), BF16)],
        compiler_params=_params("parallel", "arbitrary"),
        name="ffn_final" if final else "ffn",
    )(*operands)


def _hyena_pos_features(seq, pos_emb):
    t = jnp.linspace(0.0, 1.0, seq, dtype=F32)[:, None]
    bands = (pos_emb - 1) // 2
    w = 2.0 * math.pi * jnp.arange(seq, dtype=F32) / seq
    f = jnp.linspace(1e-4, bands - 1, bands, dtype=F32)
    ang = w[:, None] * f[None, :]
    return jnp.concatenate([t, jnp.cos(ang), -jnp.sin(ang)], axis=-1)


def _trunk(x, mod, p):
    batch, seq, d = x.shape
    depth = mod.shape[0]
    wa = p["ln_g_a"].shape[1]
    c = p["hyena_d"].shape[2]
    assert seq & (seq - 1) == 0 and seq >= 2 * PHASES * BF16_SUBLANES, "sequence length must be a power of two"
    assert wa == c and wa + c == p["w_out"].shape[1], "the two head groups must split the mixing width evenly"
    dec = seq // PHASES
    fb = min(FREQ_BLOCK, dec // 2)
    t = batch * seq
    ncols = batch * c
    x = x.reshape(t, d)
    tm = _tile(seq, MXU_TILE)
    tn = _tile(d, MXU_TILE)
    ndb = d // tn
    tile = lambda i, j: (i, j)


    table_fwd = _dft_table(dec, fb, False)
    table_inv = _dft_table(dec, fb, True)
    zpos = _hyena_pos_features(seq, p["f_w1"].shape[1])
    h3 = _filter_mlp(zpos, p["f_w1"], p["f_b1"], p["f_w2"], p["f_b2"], p["f_w3"], p["f_b3"], p["f_freq"])
    spectra = _spectrum(table_fwd, _filter_taps(h3, p["f_w_out"], p["log_decay"], c), fb)
    n_in = 2 * wa + 3 * c
    tn_in = _tile(n_in, MXU_TILE)
    tm_i = _tile(dec, MXU_TILE)
    tn_c = _tile(c, MXU_TILE * MXU_TILE // tm_i)
    ncb = c // tn_c
    tiles_per_phase = ncols // tn_c
    phase_tile = lambda i, j: (j // tiles_per_phase, i, j % tiles_per_phase)

    h = _norm_in(x, p["g_norm1"][0], mod[0], seq=seq)
    for l in range(depth):
        m = mod[l]
        zin = _matmul(h, p["w_in"], layer=l, tm=tm, tn=tn_in, tk=d, extras=[],
                      outs=[(jax.ShapeDtypeStruct((t, n_in), BF16), (tm, tn_in), tile)],
                      epilogue=_store_epilogue, name="w_in")[0]
        y = _gmlp(zin, p["ln_g_a"][l], p["ln_b_a"][l], p["w_s"][l], p["b_s"][l], p["g_grp_a"][l], out_cols=wa + c)
        v, x1, x2 = _shortconv(zin, p["conv_w"][l], p["conv_b"][l], batch=batch, seq=seq, col0=2 * wa)
        dl = p["hyena_d"][l].reshape(-1, 1, c)
        z = _poly_fwd(table_fwd, spectra, v, fb=fb, c=c, layer=l, order_idx=0)
        y1 = _matmul(table_inv, z, col_groups=PHASES, tm=tm_i, tn=tn_c, tk=CONV_TILE_K,
                     extras=[(v, (None, tm_i, tn_c), phase_tile), (x1, (None, tm_i, tn_c), phase_tile),
                             (dl, (None, 1, tn_c), lambda i, j: (0, 0, j % ncb))],
                     outs=[(jax.ShapeDtypeStruct((PHASES, dec, ncols), BF16), (None, tm_i, tn_c), phase_tile)],
                     epilogue=_gate_epilogue, name="dft_inv")[0]
        z = _poly_fwd(table_fwd, spectra, y1, fb=fb, c=c, layer=l, order_idx=1)
        y = _poly_inv_norm(table_inv, z, y1, x2, dl, p["g_grp_b"][l], y, batch=batch, order_idx=1)
        x = _matmul(y, p["w_out"], layer=l, tm=tm, tn=tn, tk=2 * MXU_TILE,
                    extras=[(x, (tm, tn), tile),
                            (m, (None, 1, tn), lambda i, j: ((i * tm) // seq, 0, 2 * ndb + j))],
                    outs=[(jax.ShapeDtypeStruct((t, d), F32), (tm, tn), tile)],
                    epilogue=_residual_epilogue, name="w_out")[0]
        if l + 1 < depth:
            x, h = _ffn(x, m, p["g_norm2"][l], p["w_up"], p["w_down"], l, p["g_norm1"][l + 1], mod[l + 1], seq=seq)
        else:
            (out,) = _ffn(x, m, p["g_norm2"][l], p["w_up"], p["w_down"], l, p["g_final"], None, seq=seq)
    return out.reshape(batch, seq, d)


def kernel(x_prompt, x_sample, c_prompt, c_sample, w_ada, b_ada, g_norm1, g_norm2, w_in, ln_g_a, ln_b_a, w_s, b_s, conv_w, conv_b, f_w1, f_b1, f_w2, f_b2, f_w3, f_b3, f_freq, f_w_out, log_decay, hyena_d, g_grp_a, g_grp_b, w_out, w_up, w_down, g_final):
    bp, bs = c_prompt.shape[0], c_sample.shape[0]
    assert bp + bs <= MOD_ROWS
    depth, d, _ = w_ada.shape
    c_all = jnp.concatenate([c_prompt, c_sample, jnp.zeros((MOD_ROWS - bp - bs, d), F32)], axis=0)
    mod = _adaln(c_all, w_ada, b_ada).reshape(depth, MOD_ROWS, 1, N_MOD * d)
    p = dict(g_norm1=g_norm1, g_norm2=g_norm2, w_in=w_in.astype(BF16), ln_g_a=ln_g_a, ln_b_a=ln_b_a,
             w_s=w_s, b_s=b_s, conv_w=conv_w, conv_b=conv_b, f_w1=f_w1, f_b1=f_b1, f_w2=f_w2, f_b2=f_b2,
             f_w3=f_w3, f_b3=f_b3, f_freq=f_freq, f_w_out=f_w_out, log_decay=log_decay, hyena_d=hyena_d,
             g_grp_a=g_grp_a, g_grp_b=g_grp_b, w_out=w_out.astype(BF16), w_up=w_up.astype(BF16),
             w_down=w_down.astype(BF16), g_final=g_final)
    y_prompt = _trunk(x_prompt, mod[:, :bp], p)
    y_sample = _trunk(x_sample, mod[:, bp:bp + bs], p)
    return (y_prompt, y_sample)
```

```python
import functools
import math

import jax
import jax.numpy as jnp
from jax import lax
from jax.experimental import pallas as pl
from jax.experimental.pallas import tpu as pltpu

F32 = jnp.float32
BF16 = jnp.bfloat16
EPS = 1e-6
N_MOD = 6
MOD_ROWS = 16
BF16_SUBLANES = 16
LANES = 128
V7X_VMEM_LIMIT_BYTES = 56 * 1024 * 1024
MXU_TILE = 1024
FREQ_BLOCK = MXU_TILE // 2
CONV_TILE_K = 2048
PHASES = 4
TAP_PAD = 8
SPECTRAL_ROWS = 256
INTERLEAVE_ROWS = 256
ROW_CHUNK = 128
HIGHEST = lax.Precision.HIGHEST
SINGLE = pl.Buffered(1)


def _tile(dim, pref):
    t = min(dim, pref)
    while dim % t:
        t //= 2
    return t


def _params(*semantics):
    return pltpu.CompilerParams(dimension_semantics=semantics,
                                vmem_limit_bytes=V7X_VMEM_LIMIT_BYTES)


def _dot(a, b):
    return jnp.dot(a, b, preferred_element_type=F32)


def _row_chunks(rows, chunk):
    chunk = _tile(rows, chunk)
    return [slice(r, r + chunk) for r in range(0, rows, chunk)]


def _rms(x):
    return x * lax.rsqrt(jnp.mean(x * x, axis=-1, keepdims=True) + EPS)


def _matmul_body(*refs, nk, n_extra, n_out, has_alias, epilogue):
    a_ref, b_ref = refs[0], refs[1]
    extra_refs = refs[2:2 + n_extra]
    first_out = 2 + n_extra + (1 if has_alias else 0)
    out_refs = refs[first_out:first_out + n_out]
    i, j, k = pl.program_id(0), pl.program_id(1), pl.program_id(2)
    if nk == 1:
        epilogue(lambda rows: _dot(a_ref[rows, :], b_ref[...]), i, j, extra_refs, out_refs)
        return
    acc_ref = refs[-1]

    @pl.when(k == 0)
    def _():
        acc_ref[...] = _dot(a_ref[...], b_ref[...])

    if nk > 2:
        @pl.when((k > 0) & (k < nk - 1))
        def _():
            acc_ref[...] += _dot(a_ref[...], b_ref[...])

    @pl.when(k == nk - 1)
    def _():
        epilogue(lambda rows: acc_ref[rows, :] + _dot(a_ref[rows, :], b_ref[...]), i, j, extra_refs, out_refs)


def _matmul(a, b, *, tm, tn, tk, extras, outs, epilogue, name, layer=None, col_groups=None, alias=None):
    m, kdim = a.shape
    n = b.shape[-1] * (col_groups or 1)
    assert b.shape[-2] == kdim
    tm, tn, tk = _tile(m, tm), _tile(b.shape[-1], tn), _tile(kdim, tk)
    nk = kdim // tk
    in_specs = [pl.BlockSpec((tm, tk), lambda i, j, k: (i, k))]
    if col_groups is not None:
        per_group = b.shape[-1] // tn
        in_specs.append(pl.BlockSpec((None, tk, tn), lambda i, j, k: (j // per_group, k, j % per_group)))
    elif layer is None:
        in_specs.append(pl.BlockSpec((tk, tn), lambda i, j, k: (k, j)))
    else:
        in_specs.append(pl.BlockSpec((None, tk, tn), lambda i, j, k: (layer, k, j)))

    def lift(imap):
        return lambda i, j, k: imap(i, j)

    in_specs += [pl.BlockSpec(blk, lift(imap)) for _, blk, imap in extras]
    operands = [a, b] + [e for e, _, _ in extras]
    aliases = {}
    if alias is not None:
        aliases = {len(operands): 0}
        in_specs.append(pl.BlockSpec(memory_space=pl.ANY))
        operands.append(alias)
    body = functools.partial(_matmul_body, nk=nk, n_extra=len(extras), n_out=len(outs),
                             has_alias=alias is not None, epilogue=epilogue)
    return pl.pallas_call(
        body,
        grid=(m // tm, n // tn, nk),
        in_specs=in_specs,
        out_specs=[pl.BlockSpec(blk, lift(imap)) for _, blk, imap in outs],
        out_shape=[o for o, _, _ in outs],
        scratch_shapes=[pltpu.VMEM((tm, tn), F32)] if nk > 1 else [],
        input_output_aliases=aliases,
        compiler_params=_params("parallel", "parallel", "arbitrary"),
        name=name,
    )(*operands)


def _store_epilogue(acc_of, i, j, extra_refs, out_refs):
    (o_ref,) = out_refs
    for rows in _row_chunks(o_ref.shape[0], o_ref.shape[0] // 2):
        o_ref[rows, :] = acc_of(rows).astype(o_ref.dtype)


def _residual_epilogue(acc_of, i, j, extra_refs, out_refs):
    x_ref, g_ref = extra_refs
    (o_ref,) = out_refs
    for rows in _row_chunks(o_ref.shape[0], o_ref.shape[0] // 2):
        o_ref[rows, :] = x_ref[rows, :] + g_ref[...] * acc_of(rows)


def _adaln_body(c_ref, w_ref, b_ref, o_ref):
    c = c_ref[...]
    cs = (c / (1.0 + jnp.exp(-c))).astype(BF16)
    o_ref[...] = _dot(cs, w_ref[...].astype(BF16)) + b_ref[...]


def _adaln(c_pad, w_ada, b_ada):
    depth, d, n = w_ada.shape
    tn = _tile(n, 512)
    return pl.pallas_call(
        _adaln_body,
        grid=(depth, n // tn),
        in_specs=[pl.BlockSpec((MOD_ROWS, d), lambda l, j: (0, 0)),
                  pl.BlockSpec((None, d, tn), lambda l, j: (l, 0, j)),
                  pl.BlockSpec((None, 1, tn), lambda l, j: (l, 0, j))],
        out_specs=pl.BlockSpec((None, MOD_ROWS, tn), lambda l, j: (l, 0, j)),
        out_shape=jax.ShapeDtypeStruct((depth, MOD_ROWS, n), F32),
        compiler_params=_params("parallel", "parallel"),
        name="adaln",
    )(c_pad, w_ada, b_ada.reshape(depth, 1, n))


def _norm_in_body(x_ref, g_ref, sc_ref, sh_ref, o_ref):
    o_ref[...] = (_rms(x_ref[...]) * g_ref[...] * (1.0 + sc_ref[...]) + sh_ref[...]).astype(o_ref.dtype)


def _norm_in(x, gain, mod, *, seq):
    t, d = x.shape
    tr = _tile(seq, ROW_CHUNK)
    tile = pl.BlockSpec((tr, d), lambda i: (i, 0))

    def mod_spec(chunk):
        return pl.BlockSpec((None, 1, d), lambda i: ((i * tr) // seq, 0, chunk))

    return pl.pallas_call(
        _norm_in_body,
        grid=(t // tr,),
        in_specs=[tile, pl.BlockSpec((1, d), lambda i: (0, 0)), mod_spec(1), mod_spec(0)],
        out_specs=tile,
        out_shape=jax.ShapeDtypeStruct((t, d), BF16),
        compiler_params=_params("parallel"),
        name="norm_in",
    )(x, gain.reshape(1, d), mod, mod)


def _gmlp_body(z_ref, lng_ref, lnb_ref, ws_ref, bs_ref, gg_ref, o_ref, gate_ref, *, chunk, heads, hd):
    wa = heads * hd
    z = z_ref[...].astype(F32)
    z = 0.5 * z * (1.0 + jnp.tanh(math.sqrt(2.0 / math.pi) * (z + 0.044715 * (z * z * z))))
    u = z[:, :wa]
    v = z[:, wa:]
    vc = v - jnp.mean(v, axis=-1, keepdims=True)
    vn = vc * lax.rsqrt(jnp.mean(vc * vc, axis=-1, keepdims=True) + EPS) * lng_ref[...] + lnb_ref[...]
    vb = vn.astype(BF16)
    for rows in _row_chunks(z.shape[0], chunk):
        for h in range(heads):
            cols = slice(h * hd, (h + 1) * hd)
            gate_ref[rows, cols] = _dot(ws_ref[h], vb[rows, cols]) + bs_ref[:, cols]
    o_ref[...] = (_rms(u * gate_ref[...]) * gg_ref[...]).astype(o_ref.dtype)


def _gmlp(zin, ln_g, ln_b, w_s, b_s, g_grp, *, out_cols):
    t = zin.shape[0]
    heads, chunk, _ = w_s.shape
    wa = ln_g.shape[0]
    hd = wa // heads
    tr = _tile(t, 2 * chunk)
    bs_full = jnp.repeat(b_s.T, hd, axis=1)
    vec = pl.BlockSpec((1, wa), lambda i: (0, 0))
    return pl.pallas_call(
        functools.partial(_gmlp_body, chunk=chunk, heads=heads, hd=hd),
        grid=(t // tr,),
        in_specs=[pl.BlockSpec((tr, 2 * wa), lambda i: (i, 0)), vec, vec,
                  pl.BlockSpec((heads, chunk, chunk), lambda i: (0, 0, 0)),
                  pl.BlockSpec((chunk, wa), lambda i: (0, 0)), vec],
        out_specs=pl.BlockSpec((tr, wa), lambda i: (i, 0)),
        out_shape=jax.ShapeDtypeStruct((t, out_cols), BF16),
        scratch_shapes=[pltpu.VMEM((tr, wa), F32)],
        compiler_params=_params("parallel"),
        name="gmlp",
    )(zin, ln_g.reshape(1, wa), ln_b.reshape(1, wa), w_s.astype(BF16), bs_full, g_grp.reshape(1, wa))


def _shortconv_body(*refs, tiles_per_seq):
    main, prev, nxt = refs[0:3], refs[3:6], refs[6:9]
    w_ref, b_ref = refs[9], refs[10]
    outs = refs[11:14]
    zc_scr = refs[14]
    li = pl.program_id(0) % tiles_per_seq
    tl, c = main[0].shape
    row = lax.broadcasted_iota(jnp.int32, (tl, c), 0)
    keep_prev = jnp.where(li == 0, 0.0, 1.0)
    keep_next = jnp.where(li == tiles_per_seq - 1, 0.0, 1.0)
    last = BF16_SUBLANES - 1
    for g in range(3):
        z = main[g][...].astype(F32)
        z_prev = prev[g][last:last + 1, :].astype(F32) * keep_prev
        z_next = nxt[g][0:1, :].astype(F32) * keep_next
        before = jnp.where(row == 0, z_prev, pltpu.roll(z, 1, 0))
        after = jnp.where(row == tl - 1, z_next, pltpu.roll(z, tl - 1, 0))
        cols = slice(g * c, (g + 1) * c)
        zc = b_ref[:, cols] + before * w_ref[0:1, cols] + z * w_ref[1:2, cols] + after * w_ref[2:3, cols]
        for k in range(c // LANES):
            lanes = slice(k * LANES, (k + 1) * LANES)
            zc_scr[k] = zc[:, lanes]
            for p in range(PHASES):
                outs[g][p, :, lanes] = zc_scr[k, pl.ds(p, tl // PHASES, stride=PHASES), :].astype(outs[g].dtype)


def _shortconv(zin, conv_w, conv_b, *, batch, seq, col0):
    c = conv_w.shape[1] // 3
    tl = _tile(seq, 256)
    tiles_per_seq = seq // tl
    cb0 = col0 // c
    halo = tl // BF16_SUBLANES
    n_halo = zin.shape[0] // BF16_SUBLANES

    def main_spec(g):
        return pl.BlockSpec((tl, c), lambda i: (i, cb0 + g))

    def prev_spec(g):
        return pl.BlockSpec((BF16_SUBLANES, c), lambda i: (jnp.maximum(i * halo - 1, 0), cb0 + g))

    def next_spec(g):
        return pl.BlockSpec((BF16_SUBLANES, c), lambda i: (jnp.minimum((i + 1) * halo, n_halo - 1), cb0 + g))

    out_spec = pl.BlockSpec((PHASES, tl // PHASES, c), lambda i: (0, i % tiles_per_seq, i // tiles_per_seq))
    return pl.pallas_call(
        functools.partial(_shortconv_body, tiles_per_seq=tiles_per_seq),
        grid=(batch * tiles_per_seq,),
        in_specs=[main_spec(g) for g in range(3)] + [prev_spec(g) for g in range(3)]
        + [next_spec(g) for g in range(3)]
        + [pl.BlockSpec((3, 3 * c), lambda i: (0, 0)), pl.BlockSpec((1, 3 * c), lambda i: (0, 0))],
        out_specs=[out_spec] * 3,
        out_shape=[jax.ShapeDtypeStruct((PHASES, seq // PHASES, batch * c), BF16)] * 3,
        scratch_shapes=[pltpu.VMEM((c // LANES, tl, LANES), F32)],
        compiler_params=_params("parallel"),
        name="shortconv",
    )(*([zin] * 9), conv_w, conv_b.reshape(1, 3 * c))


def _dft_table_body(o_ref, *, seq, fb, transposed):
    tr, tc = o_ref.shape
    r0 = pl.program_id(0) * tr
    c0 = pl.program_id(1) * tc
    rows = lax.broadcasted_iota(jnp.int32, (tr, tc), 0) + r0
    cols = lax.broadcasted_iota(jnp.int32, (tr, tc), 1) + c0
    spec_idx, pos = (cols, rows) if transposed else (rows, cols)
    log_fb = fb.bit_length() - 1
    blk = lax.shift_right_logical(spec_idx, log_fb + 1)
    p = spec_idx & (2 * fb - 1)
    imag = p >= fb
    f = lax.shift_left(blk, log_fb) + (p & (fb - 1))
    nyquist = imag & (f == 0)
    f_eff = jnp.where(nyquist, seq, f)
    shift = jnp.where(imag & (f != 0), seq // 2, 0)
    q = (f_eff * pos + shift) & (2 * seq - 1)
    o_ref[...] = jnp.cos(q.astype(F32) * (math.pi / seq)).astype(o_ref.dtype)


def _dft_table(seq, fb, transposed):
    shape = (seq, 2 * seq) if transposed else (2 * seq, seq)
    tr, tc = _tile(shape[0], 256), _tile(shape[1], 2048)
    return pl.pallas_call(
        functools.partial(_dft_table_body, seq=seq, fb=fb, transposed=transposed),
        grid=(shape[0] // tr, shape[1] // tc),
        out_specs=pl.BlockSpec((tr, tc), lambda i, j: (i, j)),
        out_shape=jax.ShapeDtypeStruct(shape, BF16),
        compiler_params=_params("parallel", "parallel"),
        name="dft_table_inv" if transposed else "dft_table_fwd",
    )()


def _filter_mlp_body(z_ref, w1_ref, b1_ref, w2_ref, b2_ref, w3_ref, b3_ref, fr_ref, o_ref):
    def hdot(a, b):
        return jnp.dot(a, b, precision=HIGHEST, preferred_element_type=F32)

    h = jnp.sin(fr_ref[0:1, :] * (hdot(z_ref[...], w1_ref[...]) + b1_ref[...]))
    h = jnp.sin(fr_ref[1:2, :] * (hdot(h, w2_ref[...]) + b2_ref[...]))
    o_ref[...] = jnp.sin(fr_ref[2:3, :] * (hdot(h, w3_ref[...]) + b3_ref[...]))


def _filter_mlp(zpos, f_w1, f_b1, f_w2, f_b2, f_w3, f_b3, f_freq):
    depth, pe, hid = f_w1.shape
    seq = zpos.shape[0]
    zpad = jnp.pad(zpos, ((0, 0), (0, hid - pe)))
    w1pad = jnp.pad(f_w1, ((0, 0), (0, hid - pe), (0, 0)))
    mat = pl.BlockSpec((None, hid, hid), lambda l: (l, 0, 0))
    vec = pl.BlockSpec((None, 1, hid), lambda l: (l, 0, 0))
    return pl.pallas_call(
        _filter_mlp_body,
        grid=(depth,),
        in_specs=[pl.BlockSpec((seq, hid), lambda l: (0, 0)), mat, vec, mat, vec, mat, vec,
                  pl.BlockSpec((None, 3, hid), lambda l: (l, 0, 0))],
        out_specs=pl.BlockSpec((None, seq, hid), lambda l: (l, 0, 0)),
        out_shape=jax.ShapeDtypeStruct((depth, seq, hid), F32),
        compiler_params=_params("parallel"),
        name="filter_mlp",
    )(zpad, w1pad, f_b1.reshape(depth, 1, hid), f_w2, f_b2.reshape(depth, 1, hid),
      f_w3, f_b3.reshape(depth, 1, hid), f_freq)


def _filter_taps_body(h_ref, wf_ref, wb_ref, df_ref, db_ref, o_ref, fw_scr, bw_scr):
    seq, tc = h_ref.shape[0], o_ref.shape[-1]
    dec = seq // PHASES
    h = h_ref[...]
    n = lax.broadcasted_iota(jnp.int32, (seq, tc), 0)
    t = n.astype(F32) * (1.0 / (seq - 1))

    def taps(w_ref, d_ref):
        k = _dot(h.astype(BF16), w_ref[...].astype(BF16))
        return k * jnp.exp(-t * jnp.exp(d_ref[...]))

    fwd = taps(wf_ref, df_ref)
    bwd = jnp.where(n == 0, 0.0, taps(wb_ref, db_ref))
    scale = lax.rsqrt(jnp.sum(fwd * fwd + bwd * bwd, axis=0, keepdims=True) + EPS)
    fwd, bwd = fwd * scale, bwd * scale
    head_row = lax.broadcasted_iota(jnp.int32, (TAP_PAD, tc), 0)
    head = jnp.zeros((TAP_PAD, tc), F32)
    for lag in range(1, PHASES):
        head = jnp.where(head_row == TAP_PAD - lag, bwd[lag:lag + 1, :], head)
    fw_scr[0:TAP_PAD, :] = head
    fw_scr[TAP_PAD:TAP_PAD + seq, :] = fwd
    bw_scr[0:TAP_PAD, :] = jnp.zeros((TAP_PAD, tc), F32)
    bw_scr[TAP_PAD:TAP_PAD + seq, :] = bwd
    bw_scr[TAP_PAD + seq:2 * TAP_PAD + seq, :] = jnp.zeros((TAP_PAD, tc), F32)
    first = lax.broadcasted_iota(jnp.int32, (dec, tc), 0) == 0
    for d in range(1 - PHASES, PHASES):
        pos = fw_scr[pl.ds(TAP_PAD + d, dec, stride=PHASES), :]
        neg = jnp.where(first, 0.0, bw_scr[pl.ds(TAP_PAD - d, dec, stride=PHASES), :])
        o_ref[d + PHASES - 1, 0] = (pos + neg).astype(o_ref.dtype)
        o_ref[d + PHASES - 1, 1] = (pos - neg).astype(o_ref.dtype)


def _filter_taps(h3, f_w_out, log_decay, c):
    depth, seq, hid = h3.shape
    order = f_w_out.shape[2] // (2 * c)
    tc = _tile(c, 128)
    ncb = c // tc
    nd = 2 * PHASES - 1
    ld = log_decay.reshape(depth, 1, -1)

    def wspec(direction):
        return pl.BlockSpec((None, hid, tc), lambda l, o, j: (l, 0, (2 * o + direction) * ncb + j))

    def dspec(direction):
        return pl.BlockSpec((None, 1, tc), lambda l, o, j: (l, 0, (2 * o + direction) * ncb + j))

    return pl.pallas_call(
        _filter_taps_body,
        grid=(depth, order, ncb),
        in_specs=[pl.BlockSpec((None, seq, hid), lambda l, o, j: (l, 0, 0)),
                  wspec(0), wspec(1), dspec(0), dspec(1)],
        out_specs=pl.BlockSpec((None, None, nd, 2, seq // PHASES, tc), lambda l, o, j: (l, o, 0, 0, 0, j)),
        out_shape=jax.ShapeDtypeStruct((depth, order, nd, 2, seq // PHASES, c), BF16),
        scratch_shapes=[pltpu.VMEM((seq + TAP_PAD, tc), F32), pltpu.VMEM((seq + 2 * TAP_PAD, tc), F32)],
        compiler_params=_params("parallel", "parallel", "parallel"),
        name="filter_taps",
    )(h3, f_w_out, f_w_out, ld, ld)


def _spectrum_body(t_ref, sa_ref, p_ref, *, fb, inv_len):
    r = pl.program_id(3)
    top = _dot(t_ref[0:fb, :], sa_ref[0]) * inv_len
    bot = _dot(t_ref[fb:2 * fb, :], sa_ref[1]) * inv_len
    p_ref[0:fb, :] = top.astype(p_ref.dtype)
    p_ref[fb:2 * fb, :] = bot.astype(p_ref.dtype)

    @pl.when(r == 0)
    def _():
        head = slice(0, BF16_SUBLANES)
        nyq = _dot(t_ref[fb:fb + BF16_SUBLANES, :], sa_ref[0]) * (0.5 * inv_len)
        row0 = lax.broadcasted_iota(jnp.int32, nyq.shape, 0) == 0
        p_ref[head, :] = jnp.where(row0, top[head, :] * 0.5, top[head, :]).astype(p_ref.dtype)
        p_ref[fb:fb + BF16_SUBLANES, :] = jnp.where(row0, nyq, bot[head, :]).astype(p_ref.dtype)


def _spectrum(table_fwd, taps, fb):
    depth, order, nd, _, dec, c = taps.shape
    tn = _tile(c, 2 * MXU_TILE)
    return pl.pallas_call(
        functools.partial(_spectrum_body, fb=fb, inv_len=1.0 / dec),
        grid=(depth, order * nd, c // tn, dec // fb),
        in_specs=[pl.BlockSpec((2 * fb, dec), lambda l, od, j, r: (r, 0)),
                  pl.BlockSpec((None, None, None, 2, dec, tn), lambda l, od, j, r: (l, od // nd, od % nd, 0, 0, j))],
        out_specs=pl.BlockSpec((None, None, None, 2 * fb, tn), lambda l, od, j, r: (l, od // nd, od % nd, r, j)),
        out_shape=jax.ShapeDtypeStruct((depth, order, nd, 2 * dec, c), BF16),
        compiler_params=_params("parallel", "parallel", "parallel", "parallel"),
        name="spectrum",
    )(table_fwd, taps)


def _poly_fwd_body(*refs, fb):
    nd = 2 * PHASES - 1
    t_ref = refs[0]
    x_refs = refs[1:1 + PHASES]
    g_refs = refs[1 + PHASES:1 + PHASES + nd]
    o_ref = refs[1 + PHASES + nd]
    i = pl.program_id(0)
    tm = t_ref.shape[0]
    for block0 in range(0, tm, 2 * fb):
        for chunk in _row_chunks(fb, SPECTRAL_ROWS):
            re = slice(block0 + chunk.start, block0 + chunk.stop)
            im = slice(re.start + fb, re.stop + fb)
            ar = [_dot(t_ref[re, :], x[...]).astype(BF16) for x in x_refs]
            ai = [_dot(t_ref[im, :], x[...]).astype(BF16) for x in x_refs]
            gr = [g[re, :] for g in g_refs]
            gi = [g[im, :] for g in g_refs]
            cross, diag = gi, gr
            if re.start == 0:
                row0 = (lax.broadcasted_iota(jnp.int32, gr[0].shape, 0) == 0) & (i == 0)
                cross = [jnp.where(row0, jnp.zeros_like(v), v) for v in gi]
                diag = [jnp.where(row0, vi, vr) for vr, vi in zip(gr, gi)]
            for m in range(PHASES):
                br = bi = None
                for p in range(PHASES):
                    d = m - p + PHASES - 1
                    tr = gr[d] * ar[p] - cross[d] * ai[p]
                    ti = cross[d] * ar[p] + diag[d] * ai[p]
                    br = tr if br is None else br + tr
                    bi = ti if bi is None else bi + ti
                o_ref[m, re, :] = br.astype(o_ref.dtype)
                o_ref[m, im, :] = bi.astype(o_ref.dtype)


def _poly_fwd(table_fwd, spectra, u, *, fb, c, layer, order_idx):
    _, dec, n = u.shape
    tm = _tile(2 * dec, MXU_TILE)
    tn = _tile(c, 512)
    ncol, ncb, nd = n // tn, c // tn, 2 * PHASES - 1

    def phase_spec(p):
        return pl.BlockSpec((None, dec, tn), lambda i, j: (p, 0, j))

    def g_spec(d):
        return pl.BlockSpec((None, None, None, tm, tn), lambda i, j: (layer, order_idx, d, i, j % ncb))

    return pl.pallas_call(
        functools.partial(_poly_fwd_body, fb=fb),
        grid=(2 * dec // tm, ncol),
        in_specs=[pl.BlockSpec((tm, dec), lambda i, j: (i, 0))]
        + [phase_spec(p) for p in range(PHASES)] + [g_spec(d) for d in range(nd)],
        out_specs=pl.BlockSpec((PHASES, tm, tn), lambda i, j: (0, i, j)),
        out_shape=jax.ShapeDtypeStruct((PHASES, 2 * dec, n), BF16),
        compiler_params=_params("parallel", "parallel"),
        name="poly_fwd",
    )(table_fwd, *([u] * PHASES), *([spectra] * nd))


def _poly_inv_norm_body(t_ref, z_ref, u_ref, gate_ref, d_ref, g_ref, y_alias_ref, o_ref, rows_scr):
    p = pl.program_id(2)
    tm = t_ref.shape[0]
    y = gate_ref[...].astype(F32) * (_dot(t_ref[...], z_ref[...]) + d_ref[...] * u_ref[...].astype(F32))
    yn = _rms(y) * g_ref[...]
    slabs = [slice(k * LANES, (k + 1) * LANES) for k in range(yn.shape[1] // LANES)]
    for q in range(PHASES):
        @pl.when(p == q)
        def _(q=q):
            for k, lanes in enumerate(slabs):
                rows_scr[k, pl.ds(q, tm, stride=PHASES), :] = yn[:, lanes]

    @pl.when(p == PHASES - 1)
    def _():
        for k, lanes in enumerate(slabs):
            o_ref[:, lanes] = rows_scr[k].astype(o_ref.dtype)


def _poly_inv_norm(table_inv, z, u, gate, d, g, y, *, batch, order_idx):
    dec = table_inv.shape[0]
    c = g.shape[-1]
    tm = _tile(dec, INTERLEAVE_ROWS)
    tiles = dec // tm
    phase_tile = pl.BlockSpec((None, tm, c), lambda i, b, p: (p, i, b))
    return pl.pallas_call(
        _poly_inv_norm_body,
        grid=(tiles, batch, PHASES),
        in_specs=[pl.BlockSpec((tm, 2 * dec), lambda i, b, p: (i, 0)),
                  pl.BlockSpec((None, 2 * dec, c), lambda i, b, p: (p, 0, b)),
                  phase_tile, phase_tile,
                  pl.BlockSpec((None, 1, c), lambda i, b, p: (order_idx, 0, 0)),
                  pl.BlockSpec((1, c), lambda i, b, p: (0, 0)),
                  pl.BlockSpec(memory_space=pl.ANY)],
        out_specs=pl.BlockSpec((PHASES * tm, c), lambda i, b, p: (b * tiles + i, y.shape[1] // c - 1)),
        out_shape=jax.ShapeDtypeStruct(y.shape, y.dtype),
        scratch_shapes=[pltpu.VMEM((c // LANES, PHASES * tm, LANES), F32)],
        input_output_aliases={6: 0},
        compiler_params=_params("parallel", "parallel", "arbitrary"),
        name="dft_inv_norm",
    )(table_inv, z, u, gate, d, g.reshape(1, c), y)


def _gate_epilogue(acc_of, i, j, extra_refs, out_refs):
    u_ref, gate_ref, d_ref = extra_refs
    (o_ref,) = out_refs
    for rows in _row_chunks(o_ref.shape[0], o_ref.shape[0] // 2):
        y = gate_ref[rows, :].astype(F32) * (acc_of(rows) + d_ref[...] * u_ref[rows, :].astype(F32))
        o_ref[rows, :] = y.astype(o_ref.dtype)


def _ffn_body(*refs, final):
    x_ref, gn_ref, sc_ref, sh_ref, gate_ref, wu_ref, wd_ref, gnext_ref = refs[:8]
    if final:
        (o_ref, h_scr) = refs[8:]
        acc_ref = o_ref
    else:
        scn_ref, shn_ref, acc_ref, o_ref, h_scr = refs[8:]
    f = pl.program_id(1)
    tm, d = x_ref.shape
    row_chunks = _row_chunks(tm, ROW_CHUNK)

    @pl.when(f == 0)
    def _():
        for rows in row_chunks:
            h = _rms(x_ref[rows, :]) * gn_ref[...] * (1.0 + sc_ref[...]) + sh_ref[...]
            h_scr[rows, :] = h.astype(h_scr.dtype)
        acc_ref[...] = jnp.zeros_like(acc_ref)

    hid = jnp.square(jnp.maximum(_dot(h_scr[...], wu_ref[...]), 0.0)).astype(BF16)
    tn = _tile(d, MXU_TILE)
    for n in range(d // tn):
        cols = slice(n * tn, (n + 1) * tn)
        acc_ref[:, cols] += _dot(hid, wd_ref[:, cols])

    @pl.when(f == pl.num_programs(1) - 1)
    def _():
        for rows in row_chunks:
            xn = x_ref[rows, :] + gate_ref[...] * acc_ref[rows, :]
            if final:
                o_ref[rows, :] = _rms(xn) * gnext_ref[...]
            else:
                acc_ref[rows, :] = xn
                o_ref[rows, :] = (_rms(xn) * gnext_ref[...] * (1.0 + scn_ref[...]) + shn_ref[...]).astype(o_ref.dtype)


def _ffn(x, mod, g_norm, w_up, w_down, layer, g_next, mod_next, *, seq):
    t, d = x.shape
    dff = w_up.shape[2]
    tm, tf = _tile(seq, 512), _tile(dff, 512)
    final = mod_next is None

    def mod_spec(chunk):
        return pl.BlockSpec((None, 1, d), lambda i, f: ((i * tm) // seq, 0, chunk))

    vec = pl.BlockSpec((1, d), lambda i, f: (0, 0))
    tile = lambda: pl.BlockSpec((tm, d), lambda i, f: (i, 0), pipeline_mode=SINGLE)
    operands = [x, g_norm.reshape(1, d), mod, mod, mod, w_up, w_down, g_next.reshape(1, d)]
    in_specs = [tile(), vec, mod_spec(4), mod_spec(3), mod_spec(5),
                pl.BlockSpec((None, d, tf), lambda i, f: (layer, 0, f)),
                pl.BlockSpec((None, tf, d), lambda i, f: (layer, f, 0)), vec]
    if final:
        out_shape = [jax.ShapeDtypeStruct((t, d), F32)]
    else:
        operands += [mod_next, mod_next]
        in_specs += [mod_spec(1), mod_spec(0)]
        out_shape = [jax.ShapeDtypeStruct((t, d), F32), jax.ShapeDtypeStruct((t, d), BF16)]
    return pl.pallas_call(
        functools.partial(_ffn_body, final=final),
        grid=(t // tm, dff // tf),
        in_specs=in_specs,
        out_specs=[tile() for _ in out_shape],
        out_shape=out_shape,
        scratch_shapes=[pltpu.VMEM((tm, d), BF16)],
        compiler_params=_params("parallel", "arbitrary"),
        name="ffn_final" if final else "ffn",
    )(*operands)


def _hyena_pos_features(seq, pos_emb):
    t = jnp.linspace(0.0, 1.0, seq, dtype=F32)[:, None]
    bands = (pos_emb - 1) // 2
    w = 2.0 * math.pi * jnp.arange(seq, dtype=F32) / seq
    f = jnp.linspace(1e-4, bands - 1, bands, dtype=F32)
    ang = w[:, None] * f[None, :]
    return jnp.concatenate([t, jnp.cos(ang), -jnp.sin(ang)], axis=-1)


def _trunk(x, mod, p):
    batch, seq, d = x.shape
    depth = mod.shape[0]
    wa = p["ln_g_a"].shape[1]
    c = p["hyena_d"].shape[2]
    assert seq & (seq - 1) == 0 and seq >= 2 * PHASES * BF16_SUBLANES, "sequence length must be a power of two"
    assert wa == c and wa + c == p["w_out"].shape[1], "the two head groups must split the mixing width evenly"
    dec = seq // PHASES
    fb = min(FREQ_BLOCK, dec // 2)
    t = batch * seq
    ncols = batch * c
    x = x.reshape(t, d)
    tm = _tile(seq, MXU_TILE)
    tn = _tile(d, MXU_TILE)
    ndb = d // tn
    tile = lambda i, j: (i, j)


    table_fwd = _dft_table(dec, fb, False)
    table_inv = _dft_table(dec, fb, True)
    zpos = _hyena_pos_features(seq, p["f_w1"].shape[1])
    h3 = _filter_mlp(zpos, p["f_w1"], p["f_b1"], p["f_w2"], p["f_b2"], p["f_w3"], p["f_b3"], p["f_freq"])
    spectra = _spectrum(table_fwd, _filter_taps(h3, p["f_w_out"], p["log_decay"], c), fb)
    n_in = 2 * wa + 3 * c
    tn_in = _tile(n_in, MXU_TILE)
    tm_i = _tile(dec, MXU_TILE)
    tn_c = _tile(c, MXU_TILE * MXU_TILE // tm_i)
    ncb = c // tn_c
    tiles_per_phase = ncols // tn_c
    phase_tile = lambda i, j: (j // tiles_per_phase, i, j % tiles_per_phase)

    h = _norm_in(x, p["g_norm1"][0], mod[0], seq=seq)
    for l in range(depth):
        m = mod[l]
        zin = _matmul(h, p["w_in"], layer=l, tm=tm, tn=tn_in, tk=d, extras=[],
                      outs=[(jax.ShapeDtypeStruct((t, n_in), BF16), (tm, tn_in), tile)],
                      epilogue=_store_epilogue, name="w_in")[0]
        y = _gmlp(zin, p["ln_g_a"][l], p["ln_b_a"][l], p["w_s"][l], p["b_s"][l], p["g_grp_a"][l], out_cols=wa + c)
        v, x1, x2 = _shortconv(zin, p["conv_w"][l], p["conv_b"][l], batch=batch, seq=seq, col0=2 * wa)
        dl = p["hyena_d"][l].reshape(-1, 1, c)
        z = _poly_fwd(table_fwd, spectra, v, fb=fb, c=c, layer=l, order_idx=0)
        y1 = _matmul(table_inv, z, col_groups=PHASES, tm=tm_i, tn=tn_c, tk=CONV_TILE_K,
                     extras=[(v, (None, tm_i, tn_c), phase_tile), (x1, (None, tm_i, tn_c), phase_tile),
                             (dl, (None, 1, tn_c), lambda i, j: (0, 0, j % ncb))],
                     outs=[(jax.ShapeDtypeStruct((PHASES, dec, ncols), BF16), (None, tm_i, tn_c), phase_tile)],
                     epilogue=_gate_epilogue, name="dft_inv")[0]
        z = _poly_fwd(table_fwd, spectra, y1, fb=fb, c=c, layer=l, order_idx=1)
        y = _poly_inv_norm(table_inv, z, y1, x2, dl, p["g_grp_b"][l], y, batch=batch, order_idx=1)
        x = _matmul(y, p["w_out"], layer=l, tm=tm, tn=tn, tk=2 * MXU_TILE,
                    extras=[(x, (tm, tn), tile),
                            (m, (None, 1, tn), lambda i, j: ((i * tm) // seq, 0, 2 * ndb + j))],
                    outs=[(jax.ShapeDtypeStruct((t, d), F32), (tm, tn), tile)],
                    epilogue=_residual_epilogue, name="w_out")[0]
        if l + 1 < depth:
            x, h = _ffn(x, m, p["g_norm2"][l], p["w_up"], p["w_down"], l, p["g_norm1"][l + 1], mod[l + 1], seq=seq)
        else:
            (out,) = _ffn(x, m, p["g_norm2"][l], p["w_up"], p["w_down"], l, p["g_final"], None, seq=seq)
    return out.reshape(batch, seq, d)


def kernel(x_prompt, x_sample, c_prompt, c_sample, w_ada, b_ada, g_norm1, g_norm2, w_in, ln_g_a, ln_b_a, w_s, b_s, conv_w, conv_b, f_w1, f_b1, f_w2, f_b2, f_w3, f_b3, f_freq, f_w_out, log_decay, hyena_d, g_grp_a, g_grp_b, w_out, w_up, w_down, g_final):
    bp, bs = c_prompt.shape[0], c_sample.shape[0]
    assert bp + bs <= MOD_ROWS
    depth, d, _ = w_ada.shape
    c_all = jnp.concatenate([c_prompt, c_sample, jnp.zeros((MOD_ROWS - bp - bs, d), F32)], axis=0)
    mod = _adaln(c_all, w_ada, b_ada).reshape(depth, MOD_ROWS, 1, N_MOD * d)
    p = dict(g_norm1=g_norm1, g_norm2=g_norm2, w_in=w_in.astype(BF16), ln_g_a=ln_g_a, ln_b_a=ln_b_a,
             w_s=w_s, b_s=b_s, conv_w=conv_w, conv_b=conv_b, f_w1=f_w1, f_b1=f_b1, f_w2=f_w2, f_b2=f_b2,
             f_w3=f_w3, f_b3=f_b3, f_freq=f_freq, f_w_out=f_w_out, log_decay=log_decay, hyena_d=hyena_d,
             g_grp_a=g_grp_a, g_grp_b=g_grp_b, w_out=w_out.astype(BF16), w_up=w_up.astype(BF16),
             w_down=w_down.astype(BF16), g_final=g_final)
    y_prompt = _trunk(x_prompt, mod[:, :bp], p)
    y_sample = _trunk(x_sample, mod[:, bp:bp + bs], p)
    return (y_prompt, y_sample)
```

```python
import functools
import math

import jax
import jax.numpy as jnp
from jax import lax
from jax.experimental import pallas as pl
from jax.experimental.pallas import tpu as pltpu

F32 = jnp.float32
BF16 = jnp.bfloat16
EPS = 1e-6
N_MOD = 6
MOD_ROWS = 16
BF16_SUBLANES = 16
LANES = 128
V7X_VMEM_LIMIT_BYTES = 56 * 1024 * 1024
MXU_TILE = 1024
FREQ_BLOCK = MXU_TILE // 2
CONV_TILE_K = 2048
PHASES = 4
TAP_PAD = 8
SPECTRAL_ROWS = 256
INTERLEAVE_ROWS = 256
FFN_HIDDEN_CHUNK = 512
ROW_CHUNK = 128
HIGHEST = lax.Precision.HIGHEST
SINGLE = pl.Buffered(1)


def _tile(dim, pref):
    t = min(dim, pref)
    while dim % t:
        t //= 2
    return t


def _params(*semantics):
    return pltpu.CompilerParams(dimension_semantics=semantics,
                                vmem_limit_bytes=V7X_VMEM_LIMIT_BYTES)


def _dot(a, b):
    return jnp.dot(a, b, preferred_element_type=F32)


def _row_chunks(rows, chunk):
    chunk = _tile(rows, chunk)
    return [slice(r, r + chunk) for r in range(0, rows, chunk)]


def _rms(x):
    return x * lax.rsqrt(jnp.mean(x * x, axis=-1, keepdims=True) + EPS)


def _matmul_body(*refs, nk, n_extra, n_out, has_tail, epilogue):
    a_ref = refs[0]
    tail_ref = refs[1] if has_tail else a_ref
    first_b = 2 if has_tail else 1
    b_ref = refs[first_b]
    extra_refs = refs[first_b + 1:first_b + 1 + n_extra]
    first_out = first_b + 1 + n_extra
    out_refs = refs[first_out:first_out + n_out]
    i, j, k = pl.program_id(0), pl.program_id(1), pl.program_id(2)
    if nk == 1:
        epilogue(lambda rows: _dot(a_ref[rows, :], b_ref[...]), i, j, extra_refs, out_refs)
        return
    acc_ref = refs[-1]

    @pl.when(k == 0)
    def _():
        acc_ref[...] = _dot(a_ref[...], b_ref[...])

    if nk > 2:
        @pl.when((k > 0) & (k < nk - 1))
        def _():
            acc_ref[...] += _dot(a_ref[...], b_ref[...])

    @pl.when(k == nk - 1)
    def _():
        epilogue(lambda rows: acc_ref[rows, :] + _dot(tail_ref[rows, :], b_ref[...]), i, j, extra_refs, out_refs)


def _matmul(a, b, *, tm, tn, tk, extras, outs, epilogue, name, layer=None, col_groups=None, a_tail=None):
    m, kdim = a.shape
    n = b.shape[-1] * (col_groups or 1)
    tm, tn, tk = _tile(m, tm), _tile(b.shape[-1], tn), _tile(kdim, tk)
    if a_tail is None:
        assert b.shape[-2] == kdim
        nk = kdim // tk
        in_specs = [pl.BlockSpec((tm, tk), lambda i, j, k: (i, k))]
    else:
        assert a_tail.shape == a.shape and b.shape[-2] == 2 * kdim and tk == kdim
        nk = 2
        in_specs = [pl.BlockSpec((tm, tk), lambda i, j, k: (i, 0))] * 2
    if col_groups is not None:
        per_group = b.shape[-1] // tn
        in_specs.append(pl.BlockSpec((None, tk, tn), lambda i, j, k: (j // per_group, k, j % per_group)))
    elif layer is None:
        in_specs.append(pl.BlockSpec((tk, tn), lambda i, j, k: (k, j)))
    else:
        in_specs.append(pl.BlockSpec((None, tk, tn), lambda i, j, k: (layer, k, j)))

    def lift(imap):
        return lambda i, j, k: imap(i, j)

    in_specs += [pl.BlockSpec(blk, lift(imap)) for _, blk, imap in extras]
    operands = ([a] if a_tail is None else [a, a_tail]) + [b] + [e for e, _, _ in extras]
    body = functools.partial(_matmul_body, nk=nk, n_extra=len(extras), n_out=len(outs),
                             has_tail=a_tail is not None, epilogue=epilogue)
    return pl.pallas_call(
        body,
        grid=(m // tm, n // tn, nk),
        in_specs=in_specs,
        out_specs=[pl.BlockSpec(blk, lift(imap)) for _, blk, imap in outs],
        out_shape=[o for o, _, _ in outs],
        scratch_shapes=[pltpu.VMEM((tm, tn), F32)] if nk > 1 else [],
        compiler_params=_params("parallel", "parallel", "arbitrary"),
        name=name,
    )(*operands)


def _store_epilogue(acc_of, i, j, extra_refs, out_refs):
    (o_ref,) = out_refs
    for rows in _row_chunks(o_ref.shape[0], o_ref.shape[0] // 2):
        o_ref[rows, :] = acc_of(rows).astype(o_ref.dtype)


def _residual_epilogue(acc_of, i, j, extra_refs, out_refs):
    x_ref, g_ref = extra_refs
    (o_ref,) = out_refs
    for rows in _row_chunks(o_ref.shape[0], o_ref.shape[0] // 2):
        o_ref[rows, :] = x_ref[rows, :] + g_ref[...] * acc_of(rows)


def _adaln_body(c_ref, w_ref, b_ref, o_ref):
    c = c_ref[...]
    cs = (c / (1.0 + jnp.exp(-c))).astype(BF16)
    o_ref[...] = _dot(cs, w_ref[...].astype(BF16)) + b_ref[...]


def _adaln(c_pad, w_ada, b_ada):
    depth, d, n = w_ada.shape
    tn = _tile(n, 512)
    return pl.pallas_call(
        _adaln_body,
        grid=(depth, n // tn),
        in_specs=[pl.BlockSpec((MOD_ROWS, d), lambda l, j: (0, 0)),
                  pl.BlockSpec((None, d, tn), lambda l, j: (l, 0, j)),
                  pl.BlockSpec((None, 1, tn), lambda l, j: (l, 0, j))],
        out_specs=pl.BlockSpec((None, MOD_ROWS, tn), lambda l, j: (l, 0, j)),
        out_shape=jax.ShapeDtypeStruct((depth, MOD_ROWS, n), F32),
        compiler_params=_params("parallel", "parallel"),
        name="adaln",
    )(c_pad, w_ada, b_ada.reshape(depth, 1, n))


def _norm_in_body(x_ref, g_ref, sc_ref, sh_ref, o_ref):
    o_ref[...] = (_rms(x_ref[...]) * g_ref[...] * (1.0 + sc_ref[...]) + sh_ref[...]).astype(o_ref.dtype)


def _norm_in(x, gain, mod, *, seq):
    t, d = x.shape
    tr = _tile(seq, ROW_CHUNK)
    tile = pl.BlockSpec((tr, d), lambda i: (i, 0))

    def mod_spec(chunk):
        return pl.BlockSpec((None, 1, d), lambda i: ((i * tr) // seq, 0, chunk))

    return pl.pallas_call(
        _norm_in_body,
        grid=(t // tr,),
        in_specs=[tile, pl.BlockSpec((1, d), lambda i: (0, 0)), mod_spec(1), mod_spec(0)],
        out_specs=tile,
        out_shape=jax.ShapeDtypeStruct((t, d), BF16),
        compiler_params=_params("parallel"),
        name="norm_in",
    )(x, gain.reshape(1, d), mod, mod)


def _gmlp_body(z_ref, lng_ref, lnb_ref, ws_ref, bs_ref, gg_ref, o_ref, gate_ref, *, chunk, heads, hd):
    wa = heads * hd
    z = z_ref[...].astype(F32)
    z = 0.5 * z * (1.0 + jnp.tanh(math.sqrt(2.0 / math.pi) * (z + 0.044715 * (z * z * z))))
    u = z[:, :wa]
    v = z[:, wa:]
    vc = v - jnp.mean(v, axis=-1, keepdims=True)
    vn = vc * lax.rsqrt(jnp.mean(vc * vc, axis=-1, keepdims=True) + EPS) * lng_ref[...] + lnb_ref[...]
    vb = vn.astype(BF16)
    for rows in _row_chunks(z.shape[0], chunk):
        for h in range(heads):
            cols = slice(h * hd, (h + 1) * hd)
            gate_ref[rows, cols] = _dot(ws_ref[h], vb[rows, cols]) + bs_ref[:, cols]
    o_ref[...] = (_rms(u * gate_ref[...]) * gg_ref[...]).astype(o_ref.dtype)


def _gmlp(zin, ln_g, ln_b, w_s, b_s, g_grp):
    t = zin.shape[0]
    heads, chunk, _ = w_s.shape
    wa = ln_g.shape[0]
    hd = wa // heads
    tr = _tile(t, 2 * chunk)
    bs_full = jnp.repeat(b_s.T, hd, axis=1)
    vec = pl.BlockSpec((1, wa), lambda i: (0, 0))
    return pl.pallas_call(
        functools.partial(_gmlp_body, chunk=chunk, heads=heads, hd=hd),
        grid=(t // tr,),
        in_specs=[pl.BlockSpec((tr, 2 * wa), lambda i: (i, 0)), vec, vec,
                  pl.BlockSpec((heads, chunk, chunk), lambda i: (0, 0, 0)),
                  pl.BlockSpec((chunk, wa), lambda i: (0, 0)), vec],
        out_specs=pl.BlockSpec((tr, wa), lambda i: (i, 0)),
        out_shape=jax.ShapeDtypeStruct((t, wa), BF16),
        scratch_shapes=[pltpu.VMEM((tr, wa), F32)],
        compiler_params=_params("parallel"),
        name="gmlp",
    )(zin, ln_g.reshape(1, wa), ln_b.reshape(1, wa), w_s.astype(BF16), bs_full, g_grp.reshape(1, wa))


def _shortconv_body(*refs, tiles_per_seq):
    main, prev, nxt = refs[0:3], refs[3:6], refs[6:9]
    w_ref, b_ref = refs[9], refs[10]
    outs = refs[11:14]
    zc_scr = refs[14]
    li = pl.program_id(0) % tiles_per_seq
    tl, c = main[0].shape
    row = lax.broadcasted_iota(jnp.int32, (tl, c), 0)
    keep_prev = jnp.where(li == 0, 0.0, 1.0)
    keep_next = jnp.where(li == tiles_per_seq - 1, 0.0, 1.0)
    last = BF16_SUBLANES - 1
    for g in range(3):
        z = main[g][...].astype(F32)
        z_prev = prev[g][last:last + 1, :].astype(F32) * keep_prev
        z_next = nxt[g][0:1, :].astype(F32) * keep_next
        before = jnp.where(row == 0, z_prev, pltpu.roll(z, 1, 0))
        after = jnp.where(row == tl - 1, z_next, pltpu.roll(z, tl - 1, 0))
        cols = slice(g * c, (g + 1) * c)
        zc = b_ref[:, cols] + before * w_ref[0:1, cols] + z * w_ref[1:2, cols] + after * w_ref[2:3, cols]
        for k in range(c // LANES):
            lanes = slice(k * LANES, (k + 1) * LANES)
            zc_scr[k] = zc[:, lanes]
            for p in range(PHASES):
                outs[g][p, :, lanes] = zc_scr[k, pl.ds(p, tl // PHASES, stride=PHASES), :].astype(outs[g].dtype)


def _shortconv(zin, conv_w, conv_b, *, batch, seq, col0):
    c = conv_w.shape[1] // 3
    tl = _tile(seq, 256)
    tiles_per_seq = seq // tl
    cb0 = col0 // c
    halo = tl // BF16_SUBLANES
    n_halo = zin.shape[0] // BF16_SUBLANES

    def main_spec(g):
        return pl.BlockSpec((tl, c), lambda i: (i, cb0 + g))

    def prev_spec(g):
        return pl.BlockSpec((BF16_SUBLANES, c), lambda i: (jnp.maximum(i * halo - 1, 0), cb0 + g))

    def next_spec(g):
        return pl.BlockSpec((BF16_SUBLANES, c), lambda i: (jnp.minimum((i + 1) * halo, n_halo - 1), cb0 + g))

    out_spec = pl.BlockSpec((PHASES, tl // PHASES, c), lambda i: (0, i % tiles_per_seq, i // tiles_per_seq))
    return pl.pallas_call(
        functools.partial(_shortconv_body, tiles_per_seq=tiles_per_seq),
        grid=(batch * tiles_per_seq,),
        in_specs=[main_spec(g) for g in range(3)] + [prev_spec(g) for g in range(3)]
        + [next_spec(g) for g in range(3)]
        + [pl.BlockSpec((3, 3 * c), lambda i: (0, 0)), pl.BlockSpec((1, 3 * c), lambda i: (0, 0))],
        out_specs=[out_spec] * 3,
        out_shape=[jax.ShapeDtypeStruct((PHASES, seq // PHASES, batch * c), BF16)] * 3,
        scratch_shapes=[pltpu.VMEM((c // LANES, tl, LANES), F32)],
        compiler_params=_params("parallel"),
        name="shortconv",
    )(*([zin] * 9), conv_w, conv_b.reshape(1, 3 * c))


def _dft_table_body(o_ref, *, seq, fb, transposed):
    tr, tc = o_ref.shape
    r0 = pl.program_id(0) * tr
    c0 = pl.program_id(1) * tc
    rows = lax.broadcasted_iota(jnp.int32, (tr, tc), 0) + r0
    cols = lax.broadcasted_iota(jnp.int32, (tr, tc), 1) + c0
    spec_idx, pos = (cols, rows) if transposed else (rows, cols)
    log_fb = fb.bit_length() - 1
    blk = lax.shift_right_logical(spec_idx, log_fb + 1)
    p = spec_idx & (2 * fb - 1)
    imag = p >= fb
    f = lax.shift_left(blk, log_fb) + (p & (fb - 1))
    nyquist = imag & (f == 0)
    f_eff = jnp.where(nyquist, seq, f)
    shift = jnp.where(imag & (f != 0), seq // 2, 0)
    q = (f_eff * pos + shift) & (2 * seq - 1)
    o_ref[...] = jnp.cos(q.astype(F32) * (math.pi / seq)).astype(o_ref.dtype)


def _dft_table(seq, fb, transposed):
    shape = (seq, 2 * seq) if transposed else (2 * seq, seq)
    tr, tc = _tile(shape[0], 256), _tile(shape[1], 2048)
    return pl.pallas_call(
        functools.partial(_dft_table_body, seq=seq, fb=fb, transposed=transposed),
        grid=(shape[0] // tr, shape[1] // tc),
        out_specs=pl.BlockSpec((tr, tc), lambda i, j: (i, j)),
        out_shape=jax.ShapeDtypeStruct(shape, BF16),
        compiler_params=_params("parallel", "parallel"),
        name="dft_table_inv" if transposed else "dft_table_fwd",
    )()


def _filter_mlp_body(z_ref, w1_ref, b1_ref, w2_ref, b2_ref, w3_ref, b3_ref, fr_ref, o_ref):
    def hdot(a, b):
        return jnp.dot(a, b, precision=HIGHEST, preferred_element_type=F32)

    h = jnp.sin(fr_ref[0:1, :] * (hdot(z_ref[...], w1_ref[...]) + b1_ref[...]))
    h = jnp.sin(fr_ref[1:2, :] * (hdot(h, w2_ref[...]) + b2_ref[...]))
    o_ref[...] = jnp.sin(fr_ref[2:3, :] * (hdot(h, w3_ref[...]) + b3_ref[...]))


def _filter_mlp(zpos, f_w1, f_b1, f_w2, f_b2, f_w3, f_b3, f_freq):
    depth, pe, hid = f_w1.shape
    seq = zpos.shape[0]
    zpad = jnp.pad(zpos, ((0, 0), (0, hid - pe)))
    w1pad = jnp.pad(f_w1, ((0, 0), (0, hid - pe), (0, 0)))
    mat = pl.BlockSpec((None, hid, hid), lambda l: (l, 0, 0))
    vec = pl.BlockSpec((None, 1, hid), lambda l: (l, 0, 0))
    return pl.pallas_call(
        _filter_mlp_body,
        grid=(depth,),
        in_specs=[pl.BlockSpec((seq, hid), lambda l: (0, 0)), mat, vec, mat, vec, mat, vec,
                  pl.BlockSpec((None, 3, hid), lambda l: (l, 0, 0))],
        out_specs=pl.BlockSpec((None, seq, hid), lambda l: (l, 0, 0)),
        out_shape=jax.ShapeDtypeStruct((depth, seq, hid), F32),
        compiler_params=_params("parallel"),
        name="filter_mlp",
    )(zpad, w1pad, f_b1.reshape(depth, 1, hid), f_w2, f_b2.reshape(depth, 1, hid),
      f_w3, f_b3.reshape(depth, 1, hid), f_freq)


def _filter_taps_body(h_ref, wf_ref, wb_ref, df_ref, db_ref, o_ref, fw_scr, bw_scr):
    seq, tc = h_ref.shape[0], o_ref.shape[-1]
    dec = seq // PHASES
    h = h_ref[...]
    n = lax.broadcasted_iota(jnp.int32, (seq, tc), 0)
    t = n.astype(F32) * (1.0 / (seq - 1))

    def taps(w_ref, d_ref):
        k = _dot(h.astype(BF16), w_ref[...].astype(BF16))
        return k * jnp.exp(-t * jnp.exp(d_ref[...]))

    fwd = taps(wf_ref, df_ref)
    bwd = jnp.where(n == 0, 0.0, taps(wb_ref, db_ref))
    scale = lax.rsqrt(jnp.sum(fwd * fwd + bwd * bwd, axis=0, keepdims=True) + EPS)
    fwd, bwd = fwd * scale, bwd * scale
    head_row = lax.broadcasted_iota(jnp.int32, (TAP_PAD, tc), 0)
    head = jnp.zeros((TAP_PAD, tc), F32)
    for lag in range(1, PHASES):
        head = jnp.where(head_row == TAP_PAD - lag, bwd[lag:lag + 1, :], head)
    fw_scr[0:TAP_PAD, :] = head
    fw_scr[TAP_PAD:TAP_PAD + seq, :] = fwd
    bw_scr[0:TAP_PAD, :] = jnp.zeros((TAP_PAD, tc), F32)
    bw_scr[TAP_PAD:TAP_PAD + seq, :] = bwd
    bw_scr[TAP_PAD + seq:2 * TAP_PAD + seq, :] = jnp.zeros((TAP_PAD, tc), F32)
    first = lax.broadcasted_iota(jnp.int32, (dec, tc), 0) == 0
    for d in range(1 - PHASES, PHASES):
        pos = fw_scr[pl.ds(TAP_PAD + d, dec, stride=PHASES), :]
        neg = jnp.where(first, 0.0, bw_scr[pl.ds(TAP_PAD - d, dec, stride=PHASES), :])
        o_ref[d + PHASES - 1, 0] = (pos + neg).astype(o_ref.dtype)
        o_ref[d + PHASES - 1, 1] = (pos - neg).astype(o_ref.dtype)


def _filter_taps(h3, f_w_out, log_decay, c):
    depth, seq, hid = h3.shape
    order = f_w_out.shape[2] // (2 * c)
    tc = _tile(c, 128)
    ncb = c // tc
    nd = 2 * PHASES - 1
    ld = log_decay.reshape(depth, 1, -1)

    def wspec(direction):
        return pl.BlockSpec((None, hid, tc), lambda l, o, j: (l, 0, (2 * o + direction) * ncb + j))

    def dspec(direction):
        return pl.BlockSpec((None, 1, tc), lambda l, o, j: (l, 0, (2 * o + direction) * ncb + j))

    return pl.pallas_call(
        _filter_taps_body,
        grid=(depth, order, ncb),
        in_specs=[pl.BlockSpec((None, seq, hid), lambda l, o, j: (l, 0, 0)),
                  wspec(0), wspec(1), dspec(0), dspec(1)],
        out_specs=pl.BlockSpec((None, None, nd, 2, seq // PHASES, tc), lambda l, o, j: (l, o, 0, 0, 0, j)),
        out_shape=jax.ShapeDtypeStruct((depth, order, nd, 2, seq // PHASES, c), BF16),
        scratch_shapes=[pltpu.VMEM((seq + TAP_PAD, tc), F32), pltpu.VMEM((seq + 2 * TAP_PAD, tc), F32)],
        compiler_params=_params("parallel", "parallel", "parallel"),
        name="filter_taps",
    )(h3, f_w_out, f_w_out, ld, ld)


def _spectrum_body(t_ref, sa_ref, p_ref, *, fb, inv_len):
    r = pl.program_id(3)
    top = _dot(t_ref[0:fb, :], sa_ref[0]) * inv_len
    bot = _dot(t_ref[fb:2 * fb, :], sa_ref[1]) * inv_len
    p_ref[0:fb, :] = top.astype(p_ref.dtype)
    p_ref[fb:2 * fb, :] = bot.astype(p_ref.dtype)

    @pl.when(r == 0)
    def _():
        head = slice(0, BF16_SUBLANES)
        nyq = _dot(t_ref[fb:fb + BF16_SUBLANES, :], sa_ref[0]) * (0.5 * inv_len)
        row0 = lax.broadcasted_iota(jnp.int32, nyq.shape, 0) == 0
        p_ref[head, :] = jnp.where(row0, top[head, :] * 0.5, top[head, :]).astype(p_ref.dtype)
        p_ref[fb:fb + BF16_SUBLANES, :] = jnp.where(row0, nyq, bot[head, :]).astype(p_ref.dtype)


def _spectrum(table_fwd, taps, fb):
    depth, order, nd, _, dec, c = taps.shape
    tn = _tile(c, 2 * MXU_TILE)
    return pl.pallas_call(
        functools.partial(_spectrum_body, fb=fb, inv_len=1.0 / dec),
        grid=(depth, order * nd, c // tn, dec // fb),
        in_specs=[pl.BlockSpec((2 * fb, dec), lambda l, od, j, r: (r, 0)),
                  pl.BlockSpec((None, None, None, 2, dec, tn), lambda l, od, j, r: (l, od // nd, od % nd, 0, 0, j))],
        out_specs=pl.BlockSpec((None, None, None, 2 * fb, tn), lambda l, od, j, r: (l, od // nd, od % nd, r, j)),
        out_shape=jax.ShapeDtypeStruct((depth, order, nd, 2 * dec, c), BF16),
        compiler_params=_params("parallel", "parallel", "parallel", "parallel"),
        name="spectrum",
    )(table_fwd, taps)


def _poly_fwd_body(*refs, fb):
    nd = 2 * PHASES - 1
    t_ref = refs[0]
    x_refs = refs[1:1 + PHASES]
    g_refs = refs[1 + PHASES:1 + PHASES + nd]
    o_ref = refs[1 + PHASES + nd]
    i = pl.program_id(0)
    tm = t_ref.shape[0]
    for block0 in range(0, tm, 2 * fb):
        for chunk in _row_chunks(fb, SPECTRAL_ROWS):
            re = slice(block0 + chunk.start, block0 + chunk.stop)
            im = slice(re.start + fb, re.stop + fb)
            ar = [_dot(t_ref[re, :], x[...]).astype(BF16) for x in x_refs]
            ai = [_dot(t_ref[im, :], x[...]).astype(BF16) for x in x_refs]
            gr = [g[re, :] for g in g_refs]
            gi = [g[im, :] for g in g_refs]
            cross, diag = gi, gr
            if re.start == 0:
                row0 = (lax.broadcasted_iota(jnp.int32, gr[0].shape, 0) == 0) & (i == 0)
                cross = [jnp.where(row0, jnp.zeros_like(v), v) for v in gi]
                diag = [jnp.where(row0, vi, vr) for vr, vi in zip(gr, gi)]
            for m in range(PHASES):
                br = bi = None
                for p in range(PHASES):
                    d = m - p + PHASES - 1
                    tr = gr[d] * ar[p] - cross[d] * ai[p]
                    ti = cross[d] * ar[p] + diag[d] * ai[p]
                    br = tr if br is None else br + tr
                    bi = ti if bi is None else bi + ti
                o_ref[m, re, :] = br.astype(o_ref.dtype)
                o_ref[m, im, :] = bi.astype(o_ref.dtype)


def _poly_fwd(table_fwd, spectra, u, *, fb, c, layer, order_idx):
    _, dec, n = u.shape
    tm = _tile(2 * dec, MXU_TILE)
    tn = _tile(c, 512)
    ncol, ncb, nd = n // tn, c // tn, 2 * PHASES - 1

    def phase_spec(p):
        return pl.BlockSpec((None, dec, tn), lambda i, j: (p, 0, j))

    def g_spec(d):
        return pl.BlockSpec((None, None, None, tm, tn), lambda i, j: (layer, order_idx, d, i, j % ncb))

    return pl.pallas_call(
        functools.partial(_poly_fwd_body, fb=fb),
        grid=(2 * dec // tm, ncol),
        in_specs=[pl.BlockSpec((tm, dec), lambda i, j: (i, 0))]
        + [phase_spec(p) for p in range(PHASES)] + [g_spec(d) for d in range(nd)],
        out_specs=pl.BlockSpec((PHASES, tm, tn), lambda i, j: (0, i, j)),
        out_shape=jax.ShapeDtypeStruct((PHASES, 2 * dec, n), BF16),
        compiler_params=_params("parallel", "parallel"),
        name="poly_fwd",
    )(table_fwd, *([u] * PHASES), *([spectra] * nd))


def _poly_inv_norm_body(t_ref, z_ref, u_ref, gate_ref, d_ref, g_ref, o_ref, rows_scr):
    p = pl.program_id(2)
    tm = t_ref.shape[0]
    y = gate_ref[...].astype(F32) * (_dot(t_ref[...], z_ref[...]) + d_ref[...] * u_ref[...].astype(F32))
    yn = _rms(y) * g_ref[...]
    slabs = [slice(k * LANES, (k + 1) * LANES) for k in range(yn.shape[1] // LANES)]
    for q in range(PHASES):
        @pl.when(p == q)
        def _(q=q):
            for k, lanes in enumerate(slabs):
                rows_scr[k, pl.ds(q, tm, stride=PHASES), :] = yn[:, lanes]

    @pl.when(p == PHASES - 1)
    def _():
        for k, lanes in enumerate(slabs):
            o_ref[:, lanes] = rows_scr[k].astype(o_ref.dtype)


def _poly_inv_norm(table_inv, z, u, gate, d, g, *, batch, order_idx):
    dec = table_inv.shape[0]
    c = g.shape[-1]
    tm = _tile(dec, INTERLEAVE_ROWS)
    tiles = dec // tm
    phase_tile = pl.BlockSpec((None, tm, c), lambda i, b, p: (p, i, b))
    return pl.pallas_call(
        _poly_inv_norm_body,
        grid=(tiles, batch, PHASES),
        in_specs=[pl.BlockSpec((tm, 2 * dec), lambda i, b, p: (i, 0)),
                  pl.BlockSpec((None, 2 * dec, c), lambda i, b, p: (p, 0, b)),
                  phase_tile, phase_tile,
                  pl.BlockSpec((None, 1, c), lambda i, b, p: (order_idx, 0, 0)),
                  pl.BlockSpec((1, c), lambda i, b, p: (0, 0))],
        out_specs=pl.BlockSpec((PHASES * tm, c), lambda i, b, p: (b * tiles + i, 0)),
        out_shape=jax.ShapeDtypeStruct((batch * PHASES * dec, c), BF16),
        scratch_shapes=[pltpu.VMEM((c // LANES, PHASES * tm, LANES), F32)],
        compiler_params=_params("parallel", "parallel", "arbitrary"),
        name="dft_inv_norm",
    )(table_inv, z, u, gate, d, g.reshape(1, c))


def _gate_epilogue(acc_of, i, j, extra_refs, out_refs):
    u_ref, gate_ref, d_ref = extra_refs
    (o_ref,) = out_refs
    for rows in _row_chunks(o_ref.shape[0], o_ref.shape[0] // 2):
        y = gate_ref[rows, :].astype(F32) * (acc_of(rows) + d_ref[...] * u_ref[rows, :].astype(F32))
        o_ref[rows, :] = y.astype(o_ref.dtype)


def _ffn_body(*refs, final):
    x_hbm, gn_ref, sc_ref, sh_ref, gate_ref, wu_ref, wd_ref, gnext_ref = refs[:8]
    if final:
        o_ref, h_scr, x_sem = refs[8:]
        acc_ref = o_ref
    else:
        scn_ref, shn_ref, acc_ref, o_ref, h_scr, x_sem = refs[8:]
    i, f = pl.program_id(0), pl.program_id(1)
    tm, d = acc_ref.shape
    row_chunks = _row_chunks(tm, ROW_CHUNK)

    @pl.when(f == 0)
    def _():
        load_x = pltpu.make_async_copy(x_hbm.at[pl.ds(pl.multiple_of(i * tm, tm), tm), :], acc_ref, x_sem)
        load_x.start()
        load_x.wait()
        for rows in row_chunks:
            h = _rms(acc_ref[rows, :]) * gn_ref[...] * (1.0 + sc_ref[...]) + sh_ref[...]
            h_scr[rows, :] = h.astype(h_scr.dtype)

    tn = _tile(d, MXU_TILE)
    for hidden in _row_chunks(wu_ref.shape[1], FFN_HIDDEN_CHUNK):
        hid = jnp.square(jnp.maximum(_dot(h_scr[...], wu_ref[:, hidden]), 0.0)).astype(BF16)
        for n in range(d // tn):
            cols = slice(n * tn, (n + 1) * tn)
            acc_ref[:, cols] += gate_ref[:, cols] * _dot(hid, wd_ref[hidden, cols])

    @pl.when(f == pl.num_programs(1) - 1)
    def _():
        for rows in row_chunks:
            xn = acc_ref[rows, :]
            if final:
                o_ref[rows, :] = _rms(xn) * gnext_ref[...]
            else:
                o_ref[rows, :] = (_rms(xn) * gnext_ref[...] * (1.0 + scn_ref[...]) + shn_ref[...]).astype(o_ref.dtype)


def _ffn(x, mod, g_norm, w_up, w_down, layer, g_next, mod_next, *, seq):
    t, d = x.shape
    dff = w_up.shape[2]
    tm, tf = _tile(seq, 512), _tile(dff, 2 * FFN_HIDDEN_CHUNK)
    final = mod_next is None

    def mod_spec(chunk):
        return pl.BlockSpec((None, 1, d), lambda i, f: ((i * tm) // seq, 0, chunk))

    vec = pl.BlockSpec((1, d), lambda i, f: (0, 0))
    tile = lambda: pl.BlockSpec((tm, d), lambda i, f: (i, 0), pipeline_mode=SINGLE)
    operands = [x, g_norm.reshape(1, d), mod, mod, mod, w_up, w_down, g_next.reshape(1, d)]
    in_specs = [pl.BlockSpec(memory_space=pl.ANY), vec, mod_spec(4), mod_spec(3), mod_spec(5),
                pl.BlockSpec((None, d, tf), lambda i, f: (layer, 0, f)),
                pl.BlockSpec((None, tf, d), lambda i, f: (layer, f, 0)), vec]
    if final:
        out_shape = [jax.ShapeDtypeStruct((t, d), F32)]
    else:
        operands += [mod_next, mod_next]
        in_specs += [mod_spec(1), mod_spec(0)]
        out_shape = [jax.ShapeDtypeStruct((t, d), F32), jax.ShapeDtypeStruct((t, d), BF16)]
    return pl.pallas_call(
        functools.partial(_ffn_body, final=final),
        grid=(t // tm, dff // tf),
        in_specs=in_specs,
        out_specs=[tile() for _ in out_shape],
        out_shape=out_shape,
        scratch_shapes=[pltpu.VMEM((tm, d), BF16), pltpu.SemaphoreType.DMA(())],
        compiler_params=_params("parallel", "arbitrary"),
        name="ffn_final" if final else "ffn",
    )(*operands)


def _hyena_pos_features(seq, pos_emb):
    t = jnp.linspace(0.0, 1.0, seq, dtype=F32)[:, None]
    bands = (pos_emb - 1) // 2
    w = 2.0 * math.pi * jnp.arange(seq, dtype=F32) / seq
    f = jnp.linspace(1e-4, bands - 1, bands, dtype=F32)
    ang = w[:, None] * f[None, :]
    return jnp.concatenate([t, jnp.cos(ang), -jnp.sin(ang)], axis=-1)


def _trunk(x, mod, p):
    batch, seq, d = x.shape
    depth = mod.shape[0]
    wa = p["ln_g_a"].shape[1]
    c = p["hyena_d"].shape[2]
    assert seq & (seq - 1) == 0 and seq >= 2 * PHASES * BF16_SUBLANES, "sequence length must be a power of two"
    assert wa == c and wa + c == p["w_out"].shape[1], "the two head groups must split the mixing width evenly"
    dec = seq // PHASES
    fb = min(FREQ_BLOCK, dec // 2)
    t = batch * seq
    ncols = batch * c
    x = x.reshape(t, d)
    tm = _tile(seq, MXU_TILE)
    tn = _tile(d, MXU_TILE)
    ndb = d // tn
    tile = lambda i, j: (i, j)


    table_fwd = _dft_table(dec, fb, False)
    table_inv = _dft_table(dec, fb, True)
    zpos = _hyena_pos_features(seq, p["f_w1"].shape[1])
    h3 = _filter_mlp(zpos, p["f_w1"], p["f_b1"], p["f_w2"], p["f_b2"], p["f_w3"], p["f_b3"], p["f_freq"])
    spectra = _spectrum(table_fwd, _filter_taps(h3, p["f_w_out"], p["log_decay"], c), fb)
    n_in = 2 * wa + 3 * c
    tn_in = _tile(n_in, MXU_TILE)
    tm_i = _tile(dec, MXU_TILE)
    tn_c = _tile(c, MXU_TILE * MXU_TILE // tm_i)
    ncb = c // tn_c
    tiles_per_phase = ncols // tn_c
    phase_tile = lambda i, j: (j // tiles_per_phase, i, j % tiles_per_phase)

    h = _norm_in(x, p["g_norm1"][0], mod[0], seq=seq)
    for l in range(depth):
        m = mod[l]
        zin = _matmul(h, p["w_in"], layer=l, tm=tm, tn=tn_in, tk=d, extras=[],
                      outs=[(jax.ShapeDtypeStruct((t, n_in), BF16), (tm, tn_in), tile)],
                      epilogue=_store_epilogue, name="w_in")[0]
        ya = _gmlp(zin, p["ln_g_a"][l], p["ln_b_a"][l], p["w_s"][l], p["b_s"][l], p["g_grp_a"][l])
        v, x1, x2 = _shortconv(zin, p["conv_w"][l], p["conv_b"][l], batch=batch, seq=seq, col0=2 * wa)
        dl = p["hyena_d"][l].reshape(-1, 1, c)
        z = _poly_fwd(table_fwd, spectra, v, fb=fb, c=c, layer=l, order_idx=0)
        y1 = _matmul(table_inv, z, col_groups=PHASES, tm=tm_i, tn=tn_c, tk=CONV_TILE_K,
                     extras=[(v, (None, tm_i, tn_c), phase_tile), (x1, (None, tm_i, tn_c), phase_tile),
                             (dl, (None, 1, tn_c), lambda i, j: (0, 0, j % ncb))],
                     outs=[(jax.ShapeDtypeStruct((PHASES, dec, ncols), BF16), (None, tm_i, tn_c), phase_tile)],
                     epilogue=_gate_epilogue, name="dft_inv")[0]
        z = _poly_fwd(table_fwd, spectra, y1, fb=fb, c=c, layer=l, order_idx=1)
        yb = _poly_inv_norm(table_inv, z, y1, x2, dl, p["g_grp_b"][l], batch=batch, order_idx=1)
        x = _matmul(ya, p["w_out"], a_tail=yb, layer=l, tm=tm, tn=tn, tk=wa,
                    extras=[(x, (tm, tn), tile),
                            (m, (None, 1, tn), lambda i, j: ((i * tm) // seq, 0, 2 * ndb + j))],
                    outs=[(jax.ShapeDtypeStruct((t, d), F32), (tm, tn), tile)],
                    epilogue=_residual_epilogue, name="w_out")[0]
        if l + 1 < depth:
            x, h = _ffn(x, m, p["g_norm2"][l], p["w_up"], p["w_down"], l, p["g_norm1"][l + 1], mod[l + 1], seq=seq)
        else:
            (out,) = _ffn(x, m, p["g_norm2"][l], p["w_up"], p["w_down"], l, p["g_final"], None, seq=seq)
    return out.reshape(batch, seq, d)


def kernel(x_prompt, x_sample, c_prompt, c_sample, w_ada, b_ada, g_norm1, g_norm2, w_in, ln_g_a, ln_b_a, w_s, b_s, conv_w, conv_b, f_w1, f_b1, f_w2, f_b2, f_w3, f_b3, f_freq, f_w_out, log_decay, hyena_d, g_grp_a, g_grp_b, w_out, w_up, w_down, g_final):
    bp, bs = c_prompt.shape[0], c_sample.shape[0]
    assert bp + bs <= MOD_ROWS
    depth, d, _ = w_ada.shape
    c_all = jnp.concatenate([c_prompt, c_sample, jnp.zeros((MOD_ROWS - bp - bs, d), F32)], axis=0)
    mod = _adaln(c_all, w_ada, b_ada).reshape(depth, MOD_ROWS, 1, N_MOD * d)
    p = dict(g_norm1=g_norm1, g_norm2=g_norm2, w_in=w_in.astype(BF16), ln_g_a=ln_g_a, ln_b_a=ln_b_a,
             w_s=w_s, b_s=b_s, conv_w=conv_w, conv_b=conv_b, f_w1=f_w1, f_b1=f_b1, f_w2=f_w2, f_b2=f_b2,
             f_w3=f_w3, f_b3=f_b3, f_freq=f_freq, f_w_out=f_w_out, log_decay=log_decay, hyena_d=hyena_d,
             g_grp_a=g_grp_a, g_grp_b=g_grp_b, w_out=w_out.astype(BF16), w_up=w_up.astype(BF16),
             w_down=w_down.astype(BF16), g_final=g_final)
    y_prompt = _trunk(x_prompt, mod[:, :bp], p)
    y_sample = _trunk(x_sample, mod[:, bp:bp + bs], p)
    return (y_prompt, y_sample)
```

```python
import functools
import math

import jax
import jax.numpy as jnp
from jax import lax
from jax.experimental import pallas as pl
from jax.experimental.pallas import tpu as pltpu

F32 = jnp.float32
BF16 = jnp.bfloat16
EPS = 1e-6
N_MOD = 6
MOD_ROWS = 16
BF16_SUBLANES = 16
LANES = 128
V7X_VMEM_LIMIT_BYTES = 56 * 1024 * 1024
MXU_TILE = 1024
FREQ_BLOCK = MXU_TILE // 2
CONV_TILE_K = 2048
PHASES = 4
TAP_PAD = 8
SPECTRAL_ROWS = 256
INTERLEAVE_ROWS = 256
FFN_ROWS = 1024
FFN_HIDDEN_STEP = 512
FFN_HIDDEN_CHUNK = 512
FFN_COLS = 512
ROW_CHUNK = 128
HIGHEST = lax.Precision.HIGHEST
SINGLE = pl.Buffered(1)


def _tile(dim, pref):
    t = min(dim, pref)
    while dim % t:
        t //= 2
    return t


def _params(*semantics):
    return pltpu.CompilerParams(dimension_semantics=semantics,
                                vmem_limit_bytes=V7X_VMEM_LIMIT_BYTES)


def _dot(a, b):
    return jnp.dot(a, b, preferred_element_type=F32)


def _row_chunks(rows, chunk):
    chunk = _tile(rows, chunk)
    return [slice(r, r + chunk) for r in range(0, rows, chunk)]


def _rms(x):
    return x * lax.rsqrt(jnp.mean(x * x, axis=-1, keepdims=True) + EPS)


def _matmul_body(*refs, nk, n_extra, n_out, has_tail, epilogue):
    a_ref = refs[0]
    tail_ref = refs[1] if has_tail else a_ref
    first_b = 2 if has_tail else 1
    b_ref = refs[first_b]
    extra_refs = refs[first_b + 1:first_b + 1 + n_extra]
    first_out = first_b + 1 + n_extra
    out_refs = refs[first_out:first_out + n_out]
    i, j, k = pl.program_id(0), pl.program_id(1), pl.program_id(2)
    if nk == 1:
        epilogue(lambda rows: _dot(a_ref[rows, :], b_ref[...]), i, j, extra_refs, out_refs)
        return
    acc_ref = refs[-1]

    @pl.when(k == 0)
    def _():
        acc_ref[...] = _dot(a_ref[...], b_ref[...])

    if nk > 2:
        @pl.when((k > 0) & (k < nk - 1))
        def _():
            acc_ref[...] += _dot(a_ref[...], b_ref[...])

    @pl.when(k == nk - 1)
    def _():
        epilogue(lambda rows: acc_ref[rows, :] + _dot(tail_ref[rows, :], b_ref[...]), i, j, extra_refs, out_refs)


def _matmul(a, b, *, tm, tn, tk, extras, outs, epilogue, name, layer=None, col_groups=None, a_tail=None):
    m, kdim = a.shape
    n = b.shape[-1] * (col_groups or 1)
    tm, tn, tk = _tile(m, tm), _tile(b.shape[-1], tn), _tile(kdim, tk)
    if a_tail is None:
        assert b.shape[-2] == kdim
        nk = kdim // tk
        in_specs = [pl.BlockSpec((tm, tk), lambda i, j, k: (i, k))]
    else:
        assert a_tail.shape == a.shape and b.shape[-2] == 2 * kdim and tk == kdim
        nk = 2
        in_specs = [pl.BlockSpec((tm, tk), lambda i, j, k: (i, 0))] * 2
    if col_groups is not None:
        per_group = b.shape[-1] // tn
        in_specs.append(pl.BlockSpec((None, tk, tn), lambda i, j, k: (j // per_group, k, j % per_group)))
    elif layer is None:
        in_specs.append(pl.BlockSpec((tk, tn), lambda i, j, k: (k, j)))
    else:
        in_specs.append(pl.BlockSpec((None, tk, tn), lambda i, j, k: (layer, k, j)))

    def lift(imap):
        return lambda i, j, k: imap(i, j)

    in_specs += [pl.BlockSpec(blk, lift(imap)) for _, blk, imap in extras]
    operands = ([a] if a_tail is None else [a, a_tail]) + [b] + [e for e, _, _ in extras]
    body = functools.partial(_matmul_body, nk=nk, n_extra=len(extras), n_out=len(outs),
                             has_tail=a_tail is not None, epilogue=epilogue)
    return pl.pallas_call(
        body,
        grid=(m // tm, n // tn, nk),
        in_specs=in_specs,
        out_specs=[pl.BlockSpec(blk, lift(imap)) for _, blk, imap in outs],
        out_shape=[o for o, _, _ in outs],
        scratch_shapes=[pltpu.VMEM((tm, tn), F32)] if nk > 1 else [],
        compiler_params=_params("parallel", "parallel", "arbitrary"),
        name=name,
    )(*operands)


def _store_epilogue(acc_of, i, j, extra_refs, out_refs):
    (o_ref,) = out_refs
    for rows in _row_chunks(o_ref.shape[0], o_ref.shape[0] // 2):
        o_ref[rows, :] = acc_of(rows).astype(o_ref.dtype)


def _residual_epilogue(acc_of, i, j, extra_refs, out_refs):
    x_ref, g_ref = extra_refs
    (o_ref,) = out_refs
    for rows in _row_chunks(o_ref.shape[0], o_ref.shape[0] // 2):
        o_ref[rows, :] = x_ref[rows, :] + g_ref[...] * acc_of(rows)


def _adaln_body(c_ref, w_ref, b_ref, o_ref):
    c = c_ref[...]
    cs = (c / (1.0 + jnp.exp(-c))).astype(BF16)
    o_ref[...] = _dot(cs, w_ref[...].astype(BF16)) + b_ref[...]


def _adaln(c_pad, w_ada, b_ada):
    depth, d, n = w_ada.shape
    tn = _tile(n, 512)
    return pl.pallas_call(
        _adaln_body,
        grid=(depth, n // tn),
        in_specs=[pl.BlockSpec((MOD_ROWS, d), lambda l, j: (0, 0)),
                  pl.BlockSpec((None, d, tn), lambda l, j: (l, 0, j)),
                  pl.BlockSpec((None, 1, tn), lambda l, j: (l, 0, j))],
        out_specs=pl.BlockSpec((None, MOD_ROWS, tn), lambda l, j: (l, 0, j)),
        out_shape=jax.ShapeDtypeStruct((depth, MOD_ROWS, n), F32),
        compiler_params=_params("parallel", "parallel"),
        name="adaln",
    )(c_pad, w_ada, b_ada.reshape(depth, 1, n))


def _norm_in_body(x_ref, g_ref, sc_ref, sh_ref, o_ref):
    o_ref[...] = (_rms(x_ref[...]) * g_ref[...] * (1.0 + sc_ref[...]) + sh_ref[...]).astype(o_ref.dtype)


def _norm_in(x, gain, mod, *, seq):
    t, d = x.shape
    tr = _tile(seq, ROW_CHUNK)
    tile = pl.BlockSpec((tr, d), lambda i: (i, 0))

    def mod_spec(chunk):
        return pl.BlockSpec((None, 1, d), lambda i: ((i * tr) // seq, 0, chunk))

    return pl.pallas_call(
        _norm_in_body,
        grid=(t // tr,),
        in_specs=[tile, pl.BlockSpec((1, d), lambda i: (0, 0)), mod_spec(1), mod_spec(0)],
        out_specs=tile,
        out_shape=jax.ShapeDtypeStruct((t, d), BF16),
        compiler_params=_params("parallel"),
        name="norm_in",
    )(x, gain.reshape(1, d), mod, mod)


def _gmlp_body(z_ref, lng_ref, lnb_ref, ws_ref, bs_ref, gg_ref, o_ref, gate_ref, *, chunk, heads, hd):
    wa = heads * hd
    z = z_ref[...].astype(F32)
    z = 0.5 * z * (1.0 + jnp.tanh(math.sqrt(2.0 / math.pi) * (z + 0.044715 * (z * z * z))))
    u = z[:, :wa]
    v = z[:, wa:]
    vc = v - jnp.mean(v, axis=-1, keepdims=True)
    vn = vc * lax.rsqrt(jnp.mean(vc * vc, axis=-1, keepdims=True) + EPS) * lng_ref[...] + lnb_ref[...]
    vb = vn.astype(BF16)
    for rows in _row_chunks(z.shape[0], chunk):
        for h in range(heads):
            cols = slice(h * hd, (h + 1) * hd)
            gate_ref[rows, cols] = _dot(ws_ref[h], vb[rows, cols]) + bs_ref[:, cols]
    o_ref[...] = (_rms(u * gate_ref[...]) * gg_ref[...]).astype(o_ref.dtype)


def _gmlp(zin, ln_g, ln_b, w_s, b_s, g_grp):
    t = zin.shape[0]
    heads, chunk, _ = w_s.shape
    wa = ln_g.shape[0]
    hd = wa // heads
    tr = _tile(t, 2 * chunk)
    bs_full = jnp.repeat(b_s.T, hd, axis=1)
    vec = pl.BlockSpec((1, wa), lambda i: (0, 0))
    return pl.pallas_call(
        functools.partial(_gmlp_body, chunk=chunk, heads=heads, hd=hd),
        grid=(t // tr,),
        in_specs=[pl.BlockSpec((tr, 2 * wa), lambda i: (i, 0)), vec, vec,
                  pl.BlockSpec((heads, chunk, chunk), lambda i: (0, 0, 0)),
                  pl.BlockSpec((chunk, wa), lambda i: (0, 0)), vec],
        out_specs=pl.BlockSpec((tr, wa), lambda i: (i, 0)),
        out_shape=jax.ShapeDtypeStruct((t, wa), BF16),
        scratch_shapes=[pltpu.VMEM((tr, wa), F32)],
        compiler_params=_params("parallel"),
        name="gmlp",
    )(zin, ln_g.reshape(1, wa), ln_b.reshape(1, wa), w_s.astype(BF16), bs_full, g_grp.reshape(1, wa))


def _shortconv_body(*refs, tiles_per_seq):
    main, prev, nxt = refs[0:3], refs[3:6], refs[6:9]
    w_ref, b_ref = refs[9], refs[10]
    outs = refs[11:14]
    zc_scr = refs[14]
    li = pl.program_id(0) % tiles_per_seq
    tl, c = main[0].shape
    row = lax.broadcasted_iota(jnp.int32, (tl, c), 0)
    keep_prev = jnp.where(li == 0, 0.0, 1.0)
    keep_next = jnp.where(li == tiles_per_seq - 1, 0.0, 1.0)
    last = BF16_SUBLANES - 1
    for g in range(3):
        z = main[g][...].astype(F32)
        z_prev = prev[g][last:last + 1, :].astype(F32) * keep_prev
        z_next = nxt[g][0:1, :].astype(F32) * keep_next
        before = jnp.where(row == 0, z_prev, pltpu.roll(z, 1, 0))
        after = jnp.where(row == tl - 1, z_next, pltpu.roll(z, tl - 1, 0))
        cols = slice(g * c, (g + 1) * c)
        zc = b_ref[:, cols] + before * w_ref[0:1, cols] + z * w_ref[1:2, cols] + after * w_ref[2:3, cols]
        for k in range(c // LANES):
            lanes = slice(k * LANES, (k + 1) * LANES)
            zc_scr[k] = zc[:, lanes]
            for p in range(PHASES):
                outs[g][p, :, lanes] = zc_scr[k, pl.ds(p, tl // PHASES, stride=PHASES), :].astype(outs[g].dtype)


def _shortconv(zin, conv_w, conv_b, *, batch, seq, col0):
    c = conv_w.shape[1] // 3
    tl = _tile(seq, 256)
    tiles_per_seq = seq // tl
    cb0 = col0 // c
    halo = tl // BF16_SUBLANES
    n_halo = zin.shape[0] // BF16_SUBLANES

    def main_spec(g):
        return pl.BlockSpec((tl, c), lambda i: (i, cb0 + g))

    def prev_spec(g):
        return pl.BlockSpec((BF16_SUBLANES, c), lambda i: (jnp.maximum(i * halo - 1, 0), cb0 + g))

    def next_spec(g):
        return pl.BlockSpec((BF16_SUBLANES, c), lambda i: (jnp.minimum((i + 1) * halo, n_halo - 1), cb0 + g))

    out_spec = pl.BlockSpec((PHASES, tl // PHASES, c), lambda i: (0, i % tiles_per_seq, i // tiles_per_seq))
    return pl.pallas_call(
        functools.partial(_shortconv_body, tiles_per_seq=tiles_per_seq),
        grid=(batch * tiles_per_seq,),
        in_specs=[main_spec(g) for g in range(3)] + [prev_spec(g) for g in range(3)]
        + [next_spec(g) for g in range(3)]
        + [pl.BlockSpec((3, 3 * c), lambda i: (0, 0)), pl.BlockSpec((1, 3 * c), lambda i: (0, 0))],
        out_specs=[out_spec] * 3,
        out_shape=[jax.ShapeDtypeStruct((PHASES, seq // PHASES, batch * c), BF16)] * 3,
        scratch_shapes=[pltpu.VMEM((c // LANES, tl, LANES), F32)],
        compiler_params=_params("parallel"),
        name="shortconv",
    )(*([zin] * 9), conv_w, conv_b.reshape(1, 3 * c))


def _dft_table_body(o_ref, *, seq, fb, transposed):
    tr, tc = o_ref.shape
    r0 = pl.program_id(0) * tr
    c0 = pl.program_id(1) * tc
    rows = lax.broadcasted_iota(jnp.int32, (tr, tc), 0) + r0
    cols = lax.broadcasted_iota(jnp.int32, (tr, tc), 1) + c0
    spec_idx, pos = (cols, rows) if transposed else (rows, cols)
    log_fb = fb.bit_length() - 1
    blk = lax.shift_right_logical(spec_idx, log_fb + 1)
    p = spec_idx & (2 * fb - 1)
    imag = p >= fb
    f = lax.shift_left(blk, log_fb) + (p & (fb - 1))
    nyquist = imag & (f == 0)
    f_eff = jnp.where(nyquist, seq, f)
    shift = jnp.where(imag & (f != 0), seq // 2, 0)
    q = (f_eff * pos + shift) & (2 * seq - 1)
    o_ref[...] = jnp.cos(q.astype(F32) * (math.pi / seq)).astype(o_ref.dtype)


def _dft_table(seq, fb, transposed):
    shape = (seq, 2 * seq) if transposed else (2 * seq, seq)
    tr, tc = _tile(shape[0], 256), _tile(shape[1], 2048)
    return pl.pallas_call(
        functools.partial(_dft_table_body, seq=seq, fb=fb, transposed=transposed),
        grid=(shape[0] // tr, shape[1] // tc),
        out_specs=pl.BlockSpec((tr, tc), lambda i, j: (i, j)),
        out_shape=jax.ShapeDtypeStruct(shape, BF16),
        compiler_params=_params("parallel", "parallel"),
        name="dft_table_inv" if transposed else "dft_table_fwd",
    )()


def _filter_mlp_body(z_ref, w1_ref, b1_ref, w2_ref, b2_ref, w3_ref, b3_ref, fr_ref, o_ref):
    def hdot(a, b):
        return jnp.dot(a, b, precision=HIGHEST, preferred_element_type=F32)

    h = jnp.sin(fr_ref[0:1, :] * (hdot(z_ref[...], w1_ref[...]) + b1_ref[...]))
    h = jnp.sin(fr_ref[1:2, :] * (hdot(h, w2_ref[...]) + b2_ref[...]))
    o_ref[...] = jnp.sin(fr_ref[2:3, :] * (hdot(h, w3_ref[...]) + b3_ref[...]))


def _filter_mlp(zpos, f_w1, f_b1, f_w2, f_b2, f_w3, f_b3, f_freq):
    depth, pe, hid = f_w1.shape
    seq = zpos.shape[0]
    zpad = jnp.pad(zpos, ((0, 0), (0, hid - pe)))
    w1pad = jnp.pad(f_w1, ((0, 0), (0, hid - pe), (0, 0)))
    mat = pl.BlockSpec((None, hid, hid), lambda l: (l, 0, 0))
    vec = pl.BlockSpec((None, 1, hid), lambda l: (l, 0, 0))
    return pl.pallas_call(
        _filter_mlp_body,
        grid=(depth,),
        in_specs=[pl.BlockSpec((seq, hid), lambda l: (0, 0)), mat, vec, mat, vec, mat, vec,
                  pl.BlockSpec((None, 3, hid), lambda l: (l, 0, 0))],
        out_specs=pl.BlockSpec((None, seq, hid), lambda l: (l, 0, 0)),
        out_shape=jax.ShapeDtypeStruct((depth, seq, hid), F32),
        compiler_params=_params("parallel"),
        name="filter_mlp",
    )(zpad, w1pad, f_b1.reshape(depth, 1, hid), f_w2, f_b2.reshape(depth, 1, hid),
      f_w3, f_b3.reshape(depth, 1, hid), f_freq)


def _filter_taps_body(h_ref, wf_ref, wb_ref, df_ref, db_ref, o_ref, fw_scr, bw_scr):
    seq, tc = h_ref.shape[0], o_ref.shape[-1]
    dec = seq // PHASES
    h = h_ref[...]
    n = lax.broadcasted_iota(jnp.int32, (seq, tc), 0)
    t = n.astype(F32) * (1.0 / (seq - 1))

    def taps(w_ref, d_ref):
        k = _dot(h.astype(BF16), w_ref[...].astype(BF16))
        return k * jnp.exp(-t * jnp.exp(d_ref[...]))

    fwd = taps(wf_ref, df_ref)
    bwd = jnp.where(n == 0, 0.0, taps(wb_ref, db_ref))
    scale = lax.rsqrt(jnp.sum(fwd * fwd + bwd * bwd, axis=0, keepdims=True) + EPS)
    fwd, bwd = fwd * scale, bwd * scale
    head_row = lax.broadcasted_iota(jnp.int32, (TAP_PAD, tc), 0)
    head = jnp.zeros((TAP_PAD, tc), F32)
    for lag in range(1, PHASES):
        head = jnp.where(head_row == TAP_PAD - lag, bwd[lag:lag + 1, :], head)
    fw_scr[0:TAP_PAD, :] = head
    fw_scr[TAP_PAD:TAP_PAD + seq, :] = fwd
    bw_scr[0:TAP_PAD, :] = jnp.zeros((TAP_PAD, tc), F32)
    bw_scr[TAP_PAD:TAP_PAD + seq, :] = bwd
    bw_scr[TAP_PAD + seq:2 * TAP_PAD + seq, :] = jnp.zeros((TAP_PAD, tc), F32)
    first = lax.broadcasted_iota(jnp.int32, (dec, tc), 0) == 0
    for d in range(1 - PHASES, PHASES):
        pos = fw_scr[pl.ds(TAP_PAD + d, dec, stride=PHASES), :]
        neg = jnp.where(first, 0.0, bw_scr[pl.ds(TAP_PAD - d, dec, stride=PHASES), :])
        o_ref[d + PHASES - 1, 0] = (pos + neg).astype(o_ref.dtype)
        o_ref[d + PHASES - 1, 1] = (pos - neg).astype(o_ref.dtype)


def _filter_taps(h3, f_w_out, log_decay, c):
    depth, seq, hid = h3.shape
    order = f_w_out.shape[2] // (2 * c)
    tc = _tile(c, 128)
    ncb = c // tc
    nd = 2 * PHASES - 1
    ld = log_decay.reshape(depth, 1, -1)

    def wspec(direction):
        return pl.BlockSpec((None, hid, tc), lambda l, o, j: (l, 0, (2 * o + direction) * ncb + j))

    def dspec(direction):
        return pl.BlockSpec((None, 1, tc), lambda l, o, j: (l, 0, (2 * o + direction) * ncb + j))

    return pl.pallas_call(
        _filter_taps_body,
        grid=(depth, order, ncb),
        in_specs=[pl.BlockSpec((None, seq, hid), lambda l, o, j: (l, 0, 0)),
                  wspec(0), wspec(1), dspec(0), dspec(1)],
        out_specs=pl.BlockSpec((None, None, nd, 2, seq // PHASES, tc), lambda l, o, j: (l, o, 0, 0, 0, j)),
        out_shape=jax.ShapeDtypeStruct((depth, order, nd, 2, seq // PHASES, c), BF16),
        scratch_shapes=[pltpu.VMEM((seq + TAP_PAD, tc), F32), pltpu.VMEM((seq + 2 * TAP_PAD, tc), F32)],
        compiler_params=_params("parallel", "parallel", "parallel"),
        name="filter_taps",
    )(h3, f_w_out, f_w_out, ld, ld)


def _spectrum_body(t_ref, sa_ref, p_ref, *, fb, inv_len):
    r = pl.program_id(3)
    top = _dot(t_ref[0:fb, :], sa_ref[0]) * inv_len
    bot = _dot(t_ref[fb:2 * fb, :], sa_ref[1]) * inv_len
    p_ref[0:fb, :] = top.astype(p_ref.dtype)
    p_ref[fb:2 * fb, :] = bot.astype(p_ref.dtype)

    @pl.when(r == 0)
    def _():
        head = slice(0, BF16_SUBLANES)
        nyq = _dot(t_ref[fb:fb + BF16_SUBLANES, :], sa_ref[0]) * (0.5 * inv_len)
        row0 = lax.broadcasted_iota(jnp.int32, nyq.shape, 0) == 0
        p_ref[head, :] = jnp.where(row0, top[head, :] * 0.5, top[head, :]).astype(p_ref.dtype)
        p_ref[fb:fb + BF16_SUBLANES, :] = jnp.where(row0, nyq, bot[head, :]).astype(p_ref.dtype)


def _spectrum(table_fwd, taps, fb):
    depth, order, nd, _, dec, c = taps.shape
    tn = _tile(c, 2 * MXU_TILE)
    return pl.pallas_call(
        functools.partial(_spectrum_body, fb=fb, inv_len=1.0 / dec),
        grid=(depth, order * nd, c // tn, dec // fb),
        in_specs=[pl.BlockSpec((2 * fb, dec), lambda l, od, j, r: (r, 0)),
                  pl.BlockSpec((None, None, None, 2, dec, tn), lambda l, od, j, r: (l, od // nd, od % nd, 0, 0, j))],
        out_specs=pl.BlockSpec((None, None, None, 2 * fb, tn), lambda l, od, j, r: (l, od // nd, od % nd, r, j)),
        out_shape=jax.ShapeDtypeStruct((depth, order, nd, 2 * dec, c), BF16),
        compiler_params=_params("parallel", "parallel", "parallel", "parallel"),
        name="spectrum",
    )(table_fwd, taps)


def _poly_fwd_body(*refs, fb):
    nd = 2 * PHASES - 1
    t_ref = refs[0]
    x_refs = refs[1:1 + PHASES]
    g_refs = refs[1 + PHASES:1 + PHASES + nd]
    o_ref = refs[1 + PHASES + nd]
    i = pl.program_id(0)
    tm = t_ref.shape[0]
    for block0 in range(0, tm, 2 * fb):
        for chunk in _row_chunks(fb, SPECTRAL_ROWS):
            re = slice(block0 + chunk.start, block0 + chunk.stop)
            im = slice(re.start + fb, re.stop + fb)
            ar = [_dot(t_ref[re, :], x[...]).astype(BF16) for x in x_refs]
            ai = [_dot(t_ref[im, :], x[...]).astype(BF16) for x in x_refs]
            gr = [g[re, :] for g in g_refs]
            gi = [g[im, :] for g in g_refs]
            cross, diag = gi, gr
            if re.start == 0:
                row0 = (lax.broadcasted_iota(jnp.int32, gr[0].shape, 0) == 0) & (i == 0)
                cross = [jnp.where(row0, jnp.zeros_like(v), v) for v in gi]
                diag = [jnp.where(row0, vi, vr) for vr, vi in zip(gr, gi)]
            for m in range(PHASES):
                br = bi = None
                for p in range(PHASES):
                    d = m - p + PHASES - 1
                    tr = gr[d] * ar[p] - cross[d] * ai[p]
                    ti = cross[d] * ar[p] + diag[d] * ai[p]
                    br = tr if br is None else br + tr
                    bi = ti if bi is None else bi + ti
                o_ref[m, re, :] = br.astype(o_ref.dtype)
                o_ref[m, im, :] = bi.astype(o_ref.dtype)


def _poly_fwd(table_fwd, spectra, u, *, fb, c, layer, order_idx):
    _, dec, n = u.shape
    tm = _tile(2 * dec, MXU_TILE)
    tn = _tile(c, 512)
    ncol, ncb, nd = n // tn, c // tn, 2 * PHASES - 1

    def phase_spec(p):
        return pl.BlockSpec((None, dec, tn), lambda i, j: (p, 0, j))

    def g_spec(d):
        return pl.BlockSpec((None, None, None, tm, tn), lambda i, j: (layer, order_idx, d, i, j % ncb))

    return pl.pallas_call(
        functools.partial(_poly_fwd_body, fb=fb),
        grid=(2 * dec // tm, ncol),
        in_specs=[pl.BlockSpec((tm, dec), lambda i, j: (i, 0))]
        + [phase_spec(p) for p in range(PHASES)] + [g_spec(d) for d in range(nd)],
        out_specs=pl.BlockSpec((PHASES, tm, tn), lambda i, j: (0, i, j)),
        out_shape=jax.ShapeDtypeStruct((PHASES, 2 * dec, n), BF16),
        compiler_params=_params("parallel", "parallel"),
        name="poly_fwd",
    )(table_fwd, *([u] * PHASES), *([spectra] * nd))


def _poly_inv_norm_body(t_ref, z_ref, u_ref, gate_ref, d_ref, g_ref, o_ref, rows_scr):
    p = pl.program_id(2)
    tm = t_ref.shape[0]
    y = gate_ref[...].astype(F32) * (_dot(t_ref[...], z_ref[...]) + d_ref[...] * u_ref[...].astype(F32))
    yn = _rms(y) * g_ref[...]
    slabs = [slice(k * LANES, (k + 1) * LANES) for k in range(yn.shape[1] // LANES)]
    for q in range(PHASES):
        @pl.when(p == q)
        def _(q=q):
            for k, lanes in enumerate(slabs):
                rows_scr[k, pl.ds(q, tm, stride=PHASES), :] = yn[:, lanes]

    @pl.when(p == PHASES - 1)
    def _():
        for k, lanes in enumerate(slabs):
            o_ref[:, lanes] = rows_scr[k].astype(o_ref.dtype)


def _poly_inv_norm(table_inv, z, u, gate, d, g, *, batch, order_idx):
    dec = table_inv.shape[0]
    c = g.shape[-1]
    tm = _tile(dec, INTERLEAVE_ROWS)
    tiles = dec // tm
    phase_tile = pl.BlockSpec((None, tm, c), lambda i, b, p: (p, i, b))
    return pl.pallas_call(
        _poly_inv_norm_body,
        grid=(tiles, batch, PHASES),
        in_specs=[pl.BlockSpec((tm, 2 * dec), lambda i, b, p: (i, 0)),
                  pl.BlockSpec((None, 2 * dec, c), lambda i, b, p: (p, 0, b)),
                  phase_tile, phase_tile,
                  pl.BlockSpec((None, 1, c), lambda i, b, p: (order_idx, 0, 0)),
                  pl.BlockSpec((1, c), lambda i, b, p: (0, 0))],
        out_specs=pl.BlockSpec((PHASES * tm, c), lambda i, b, p: (b * tiles + i, 0)),
        out_shape=jax.ShapeDtypeStruct((batch * PHASES * dec, c), BF16),
        scratch_shapes=[pltpu.VMEM((c // LANES, PHASES * tm, LANES), F32)],
        compiler_params=_params("parallel", "parallel", "arbitrary"),
        name="dft_inv_norm",
    )(table_inv, z, u, gate, d, g.reshape(1, c))


def _gate_epilogue(acc_of, i, j, extra_refs, out_refs):
    u_ref, gate_ref, d_ref = extra_refs
    (o_ref,) = out_refs
    for rows in _row_chunks(o_ref.shape[0], o_ref.shape[0] // 2):
        y = gate_ref[rows, :].astype(F32) * (acc_of(rows) + d_ref[...] * u_ref[rows, :].astype(F32))
        o_ref[rows, :] = y.astype(o_ref.dtype)


def _ffn_body(*refs, final):
    x_hbm, gn_ref, sc_ref, sh_ref, gate_ref, wu_ref, wd_ref, gnext_ref = refs[:8]
    if final:
        o_ref, h_scr, x_sem = refs[8:]
        acc_ref = o_ref
    else:
        scn_ref, shn_ref, acc_ref, o_ref, h_scr, x_sem = refs[8:]
    i, f = pl.program_id(0), pl.program_id(1)
    tm, d = acc_ref.shape
    row_chunks = _row_chunks(tm, ROW_CHUNK)

    @pl.when(f == 0)
    def _():
        load_x = pltpu.make_async_copy(x_hbm.at[pl.ds(pl.multiple_of(i * tm, tm), tm), :], acc_ref, x_sem)
        load_x.start()
        load_x.wait()
        for rows in row_chunks:
            h = _rms(acc_ref[rows, :]) * gn_ref[...] * (1.0 + sc_ref[...]) + sh_ref[...]
            h_scr[rows, :] = h.astype(h_scr.dtype)

    tn = _tile(d, FFN_COLS)
    for hidden in _row_chunks(wu_ref.shape[1], FFN_HIDDEN_CHUNK):
        hid = jnp.square(jnp.maximum(_dot(h_scr[...], wu_ref[:, hidden]), 0.0)).astype(BF16)
        for n in range(d // tn):
            cols = slice(n * tn, (n + 1) * tn)
            acc_ref[:, cols] += gate_ref[:, cols] * _dot(hid, wd_ref[hidden, cols])

    @pl.when(f == pl.num_programs(1) - 1)
    def _():
        for rows in row_chunks:
            xn = acc_ref[rows, :]
            if final:
                o_ref[rows, :] = _rms(xn) * gnext_ref[...]
            else:
                o_ref[rows, :] = (_rms(xn) * gnext_ref[...] * (1.0 + scn_ref[...]) + shn_ref[...]).astype(o_ref.dtype)


def _ffn(x, mod, g_norm, w_up, w_down, layer, g_next, mod_next, *, seq):
    t, d = x.shape
    dff = w_up.shape[2]
    tm, tf = _tile(seq, FFN_ROWS), _tile(dff, FFN_HIDDEN_STEP)
    final = mod_next is None

    def mod_spec(chunk):
        return pl.BlockSpec((None, 1, d), lambda i, f: ((i * tm) // seq, 0, chunk))

    vec = pl.BlockSpec((1, d), lambda i, f: (0, 0))
    tile = lambda: pl.BlockSpec((tm, d), lambda i, f: (i, 0), pipeline_mode=SINGLE)
    operands = [x, g_norm.reshape(1, d), mod, mod, mod, w_up, w_down, g_next.reshape(1, d)]
    in_specs = [pl.BlockSpec(memory_space=pl.ANY), vec, mod_spec(4), mod_spec(3), mod_spec(5),
                pl.BlockSpec((None, d, tf), lambda i, f: (layer, 0, f)),
                pl.BlockSpec((None, tf, d), lambda i, f: (layer, f, 0)), vec]
    if final:
        out_shape = [jax.ShapeDtypeStruct((t, d), F32)]
    else:
        operands += [mod_next, mod_next]
        in_specs += [mod_spec(1), mod_spec(0)]
        out_shape = [jax.ShapeDtypeStruct((t, d), F32), jax.ShapeDtypeStruct((t, d), BF16)]
    return pl.pallas_call(
        functools.partial(_ffn_body, final=final),
        grid=(t // tm, dff // tf),
        in_specs=in_specs,
        out_specs=[tile() for _ in out_shape],
        out_shape=out_shape,
        scratch_shapes=[pltpu.VMEM((tm, d), BF16), pltpu.SemaphoreType.DMA(())],
        compiler_params=_params("parallel", "arbitrary"),
        name="ffn_final" if final else "ffn",
    )(*operands)


def _hyena_pos_features(seq, pos_emb):
    t = jnp.linspace(0.0, 1.0, seq, dtype=F32)[:, None]
    bands = (pos_emb - 1) // 2
    w = 2.0 * math.pi * jnp.arange(seq, dtype=F32) / seq
    f = jnp.linspace(1e-4, bands - 1, bands, dtype=F32)
    ang = w[:, None] * f[None, :]
    return jnp.concatenate([t, jnp.cos(ang), -jnp.sin(ang)], axis=-1)


def _trunk(x, mod, p):
    batch, seq, d = x.shape
    depth = mod.shape[0]
    wa = p["ln_g_a"].shape[1]
    c = p["hyena_d"].shape[2]
    assert seq & (seq - 1) == 0 and seq >= 2 * PHASES * BF16_SUBLANES, "sequence length must be a power of two"
    assert wa == c and wa + c == p["w_out"].shape[1], "the two head groups must split the mixing width evenly"
    dec = seq // PHASES
    fb = min(FREQ_BLOCK, dec // 2)
    t = batch * seq
    ncols = batch * c
    x = x.reshape(t, d)
    tm = _tile(seq, MXU_TILE)
    tn = _tile(d, MXU_TILE)
    ndb = d // tn
    tile = lambda i, j: (i, j)


    table_fwd = _dft_table(dec, fb, False)
    table_inv = _dft_table(dec, fb, True)
    zpos = _hyena_pos_features(seq, p["f_w1"].shape[1])
    h3 = _filter_mlp(zpos, p["f_w1"], p["f_b1"], p["f_w2"], p["f_b2"], p["f_w3"], p["f_b3"], p["f_freq"])
    spectra = _spectrum(table_fwd, _filter_taps(h3, p["f_w_out"], p["log_decay"], c), fb)
    n_in = 2 * wa + 3 * c
    tn_in = _tile(n_in, MXU_TILE)
    tm_i = _tile(dec, MXU_TILE)
    tn_c = _tile(c, MXU_TILE * MXU_TILE // tm_i)
    ncb = c // tn_c
    tiles_per_phase = ncols // tn_c
    phase_tile = lambda i, j: (j // tiles_per_phase, i, j % tiles_per_phase)

    h = _norm_in(x, p["g_norm1"][0], mod[0], seq=seq)
    for l in range(depth):
        m = mod[l]
        zin = _matmul(h, p["w_in"], layer=l, tm=tm, tn=tn_in, tk=d, extras=[],
                      outs=[(jax.ShapeDtypeStruct((t, n_in), BF16), (tm, tn_in), tile)],
                      epilogue=_store_epilogue, name="w_in")[0]
        ya = _gmlp(zin, p["ln_g_a"][l], p["ln_b_a"][l], p["w_s"][l], p["b_s"][l], p["g_grp_a"][l])
        v, x1, x2 = _shortconv(zin, p["conv_w"][l], p["conv_b"][l], batch=batch, seq=seq, col0=2 * wa)
        dl = p["hyena_d"][l].reshape(-1, 1, c)
        z = _poly_fwd(table_fwd, spectra, v, fb=fb, c=c, layer=l, order_idx=0)
        y1 = _matmul(table_inv, z, col_groups=PHASES, tm=tm_i, tn=tn_c, tk=CONV_TILE_K,
                     extras=[(v, (None, tm_i, tn_c), phase_tile), (x1, (None, tm_i, tn_c), phase_tile),
                             (dl, (None, 1, tn_c), lambda i, j: (0, 0, j % ncb))],
                     outs=[(jax.ShapeDtypeStruct((PHASES, dec, ncols), BF16), (None, tm_i, tn_c), phase_tile)],
                     epilogue=_gate_epilogue, name="dft_inv")[0]
        z = _poly_fwd(table_fwd, spectra, y1, fb=fb, c=c, layer=l, order_idx=1)
        yb = _poly_inv_norm(table_inv, z, y1, x2, dl, p["g_grp_b"][l], batch=batch, order_idx=1)
        x = _matmul(ya, p["w_out"], a_tail=yb, layer=l, tm=tm, tn=tn, tk=wa,
                    extras=[(x, (tm, tn), tile),
                            (m, (None, 1, tn), lambda i, j: ((i * tm) // seq, 0, 2 * ndb + j))],
                    outs=[(jax.ShapeDtypeStruct((t, d), F32), (tm, tn), tile)],
                    epilogue=_residual_epilogue, name="w_out")[0]
        if l + 1 < depth:
            x, h = _ffn(x, m, p["g_norm2"][l], p["w_up"], p["w_down"], l, p["g_norm1"][l + 1], mod[l + 1], seq=seq)
        else:
            (out,) = _ffn(x, m, p["g_norm2"][l], p["w_up"], p["w_down"], l, p["g_final"], None, seq=seq)
    return out.reshape(batch, seq, d)


def kernel(x_prompt, x_sample, c_prompt, c_sample, w_ada, b_ada, g_norm1, g_norm2, w_in, ln_g_a, ln_b_a, w_s, b_s, conv_w, conv_b, f_w1, f_b1, f_w2, f_b2, f_w3, f_b3, f_freq, f_w_out, log_decay, hyena_d, g_grp_a, g_grp_b, w_out, w_up, w_down, g_final):
    bp, bs = c_prompt.shape[0], c_sample.shape[0]
    assert bp + bs <= MOD_ROWS
    depth, d, _ = w_ada.shape
    c_all = jnp.concatenate([c_prompt, c_sample, jnp.zeros((MOD_ROWS - bp - bs, d), F32)], axis=0)
    mod = _adaln(c_all, w_ada, b_ada).reshape(depth, MOD_ROWS, 1, N_MOD * d)
    p = dict(g_norm1=g_norm1, g_norm2=g_norm2, w_in=w_in.astype(BF16), ln_g_a=ln_g_a, ln_b_a=ln_b_a,
             w_s=w_s, b_s=b_s, conv_w=conv_w, conv_b=conv_b, f_w1=f_w1, f_b1=f_b1, f_w2=f_w2, f_b2=f_b2,
             f_w3=f_w3, f_b3=f_b3, f_freq=f_freq, f_w_out=f_w_out, log_decay=log_decay, hyena_d=hyena_d,
             g_grp_a=g_grp_a, g_grp_b=g_grp_b, w_out=w_out.astype(BF16), w_up=w_up.astype(BF16),
             w_down=w_down.astype(BF16), g_final=g_final)
    y_prompt = _trunk(x_prompt, mod[:, :bp], p)
    y_sample = _trunk(x_sample, mod[:, bp:bp + bs], p)
    return (y_prompt, y_sample)
```

```python
import functools
import math

import jax
import jax.numpy as jnp
from jax import lax
from jax.experimental import pallas as pl
from jax.experimental.pallas import tpu as pltpu

F32 = jnp.float32
BF16 = jnp.bfloat16
EPS = 1e-6
N_MOD = 6
MOD_ROWS = 16
BF16_SUBLANES = 16
LANES = 128
V7X_VMEM_LIMIT_BYTES = 56 * 1024 * 1024
MXU_TILE = 1024
FREQ_BLOCK = MXU_TILE // 2
CONV_TILE_K = 2048
PHASES = 4
TAP_PAD = 8
SPECTRAL_ROWS = 256
INTERLEAVE_ROWS = 256
ADALN_COLS = 512
SHORTCONV_ROWS = 256
SPECTRAL_COLS = 512
TABLE_TILE = (256, 2048)
FFN_ROWS = 1024
FFN_HIDDEN_STEP = 512
FFN_HIDDEN_CHUNK = 512
FFN_COLS = 512
ROW_CHUNK = 128
HIGHEST = lax.Precision.HIGHEST
SINGLE = pl.Buffered(1)


def _tile(dim, pref):
    t = min(dim, pref)
    while dim % t:
        t //= 2
    return t


def _params(*semantics):
    return pltpu.CompilerParams(dimension_semantics=semantics,
                                vmem_limit_bytes=V7X_VMEM_LIMIT_BYTES)


def _dot(a, b):
    return jnp.dot(a, b, preferred_element_type=F32)


def _row_chunks(rows, chunk):
    chunk = _tile(rows, chunk)
    return [slice(r, r + chunk) for r in range(0, rows, chunk)]


def _rms(x):
    return x * lax.rsqrt(jnp.mean(x * x, axis=-1, keepdims=True) + EPS)


def _matmul_body(*refs, nk, n_extra, n_out, has_tail, epilogue):
    a_ref = refs[0]
    tail_ref = refs[1] if has_tail else a_ref
    first_b = 2 if has_tail else 1
    b_ref = refs[first_b]
    extra_refs = refs[first_b + 1:first_b + 1 + n_extra]
    first_out = first_b + 1 + n_extra
    out_refs = refs[first_out:first_out + n_out]
    i, j, k = pl.program_id(0), pl.program_id(1), pl.program_id(2)
    if nk == 1:
        epilogue(lambda rows: _dot(a_ref[rows, :], b_ref[...]), i, j, extra_refs, out_refs)
        return
    acc_ref = refs[-1]

    @pl.when(k == 0)
    def _():
        acc_ref[...] = _dot(a_ref[...], b_ref[...])

    if nk > 2:
        @pl.when((k > 0) & (k < nk - 1))
        def _():
            acc_ref[...] += _dot(a_ref[...], b_ref[...])

    @pl.when(k == nk - 1)
    def _():
        epilogue(lambda rows: acc_ref[rows, :] + _dot(tail_ref[rows, :], b_ref[...]), i, j, extra_refs, out_refs)


def _matmul(a, b, *, tm, tn, tk, extras, outs, epilogue, name, layer=None, col_groups=None, a_tail=None):
    m, kdim = a.shape
    n = b.shape[-1] * (col_groups or 1)
    tm, tn, tk = _tile(m, tm), _tile(b.shape[-1], tn), _tile(kdim, tk)
    if a_tail is None:
        assert b.shape[-2] == kdim
        nk = kdim // tk
        in_specs = [pl.BlockSpec((tm, tk), lambda i, j, k: (i, k))]
    else:
        assert a_tail.shape == a.shape and b.shape[-2] == 2 * kdim and tk == kdim
        nk = 2
        in_specs = [pl.BlockSpec((tm, tk), lambda i, j, k: (i, 0))] * 2
    if col_groups is not None:
        per_group = b.shape[-1] // tn
        in_specs.append(pl.BlockSpec((None, tk, tn), lambda i, j, k: (j // per_group, k, j % per_group)))
    elif layer is None:
        in_specs.append(pl.BlockSpec((tk, tn), lambda i, j, k: (k, j)))
    else:
        in_specs.append(pl.BlockSpec((None, tk, tn), lambda i, j, k: (layer, k, j)))

    def lift(imap):
        return lambda i, j, k: imap(i, j)

    in_specs += [pl.BlockSpec(blk, lift(imap)) for _, blk, imap in extras]
    operands = ([a] if a_tail is None else [a, a_tail]) + [b] + [e for e, _, _ in extras]
    body = functools.partial(_matmul_body, nk=nk, n_extra=len(extras), n_out=len(outs),
                             has_tail=a_tail is not None, epilogue=epilogue)
    return pl.pallas_call(
        body,
        grid=(m // tm, n // tn, nk),
        in_specs=in_specs,
        out_specs=[pl.BlockSpec(blk, lift(imap)) for _, blk, imap in outs],
        out_shape=[o for o, _, _ in outs],
        scratch_shapes=[pltpu.VMEM((tm, tn), F32)] if nk > 1 else [],
        compiler_params=_params("parallel", "parallel", "arbitrary"),
        name=name,
    )(*operands)


def _store_epilogue(acc_of, i, j, extra_refs, out_refs):
    (o_ref,) = out_refs
    for rows in _row_chunks(o_ref.shape[0], o_ref.shape[0] // 2):
        o_ref[rows, :] = acc_of(rows).astype(o_ref.dtype)


def _residual_epilogue(acc_of, i, j, extra_refs, out_refs):
    x_ref, g_ref = extra_refs
    (o_ref,) = out_refs
    for rows in _row_chunks(o_ref.shape[0], o_ref.shape[0] // 2):
        o_ref[rows, :] = x_ref[rows, :] + g_ref[...] * acc_of(rows)


def _adaln_body(c_ref, w_ref, b_ref, o_ref):
    c = c_ref[...]
    cs = (c / (1.0 + jnp.exp(-c))).astype(BF16)
    o_ref[...] = _dot(cs, w_ref[...].astype(BF16)) + b_ref[...]


def _adaln(c_pad, w_ada, b_ada):
    depth, d, n = w_ada.shape
    tn = _tile(n, ADALN_COLS)
    return pl.pallas_call(
        _adaln_body,
        grid=(depth, n // tn),
        in_specs=[pl.BlockSpec((MOD_ROWS, d), lambda l, j: (0, 0)),
                  pl.BlockSpec((None, d, tn), lambda l, j: (l, 0, j)),
                  pl.BlockSpec((None, 1, tn), lambda l, j: (l, 0, j))],
        out_specs=pl.BlockSpec((None, MOD_ROWS, tn), lambda l, j: (l, 0, j)),
        out_shape=jax.ShapeDtypeStruct((depth, MOD_ROWS, n), F32),
        compiler_params=_params("parallel", "parallel"),
        name="adaln",
    )(c_pad, w_ada, b_ada.reshape(depth, 1, n))


def _norm_in_body(x_ref, g_ref, sc_ref, sh_ref, o_ref):
    o_ref[...] = (_rms(x_ref[...]) * g_ref[...] * (1.0 + sc_ref[...]) + sh_ref[...]).astype(o_ref.dtype)


def _norm_in(x, gain, mod, *, seq):
    t, d = x.shape
    tr = _tile(seq, ROW_CHUNK)
    tile = pl.BlockSpec((tr, d), lambda i: (i, 0))

    def mod_spec(chunk):
        return pl.BlockSpec((None, 1, d), lambda i: ((i * tr) // seq, 0, chunk))

    return pl.pallas_call(
        _norm_in_body,
        grid=(t // tr,),
        in_specs=[tile, pl.BlockSpec((1, d), lambda i: (0, 0)), mod_spec(1), mod_spec(0)],
        out_specs=tile,
        out_shape=jax.ShapeDtypeStruct((t, d), BF16),
        compiler_params=_params("parallel"),
        name="norm_in",
    )(x, gain.reshape(1, d), mod, mod)


def _gmlp_body(z_ref, lng_ref, lnb_ref, ws_ref, bs_ref, gg_ref, o_ref, gate_ref, *, chunk, heads, hd):
    wa = heads * hd
    z = z_ref[...].astype(F32)
    z = 0.5 * z * (1.0 + jnp.tanh(math.sqrt(2.0 / math.pi) * (z + 0.044715 * (z * z * z))))
    u = z[:, :wa]
    v = z[:, wa:]
    vc = v - jnp.mean(v, axis=-1, keepdims=True)
    vn = vc * lax.rsqrt(jnp.mean(vc * vc, axis=-1, keepdims=True) + EPS) * lng_ref[...] + lnb_ref[...]
    vb = vn.astype(BF16)
    for rows in _row_chunks(z.shape[0], chunk):
        for h in range(heads):
            cols = slice(h * hd, (h + 1) * hd)
            gate_ref[rows, cols] = _dot(ws_ref[h], vb[rows, cols]) + bs_ref[:, cols]
    o_ref[...] = (_rms(u * gate_ref[...]) * gg_ref[...]).astype(o_ref.dtype)


def _gmlp(zin, ln_g, ln_b, w_s, b_s, g_grp):
    t = zin.shape[0]
    heads, chunk, _ = w_s.shape
    wa = ln_g.shape[0]
    hd = wa // heads
    tr = _tile(t, 2 * chunk)
    bs_full = jnp.repeat(b_s.T, hd, axis=1)
    vec = pl.BlockSpec((1, wa), lambda i: (0, 0))
    return pl.pallas_call(
        functools.partial(_gmlp_body, chunk=chunk, heads=heads, hd=hd),
        grid=(t // tr,),
        in_specs=[pl.BlockSpec((tr, 2 * wa), lambda i: (i, 0)), vec, vec,
                  pl.BlockSpec((heads, chunk, chunk), lambda i: (0, 0, 0)),
                  pl.BlockSpec((chunk, wa), lambda i: (0, 0)), vec],
        out_specs=pl.BlockSpec((tr, wa), lambda i: (i, 0)),
        out_shape=jax.ShapeDtypeStruct((t, wa), BF16),
        scratch_shapes=[pltpu.VMEM((tr, wa), F32)],
        compiler_params=_params("parallel"),
        name="gmlp",
    )(zin, ln_g.reshape(1, wa), ln_b.reshape(1, wa), w_s.astype(BF16), bs_full, g_grp.reshape(1, wa))


def _shortconv_body(*refs, tiles_per_seq):
    main, prev, nxt = refs[0:3], refs[3:6], refs[6:9]
    w_ref, b_ref = refs[9], refs[10]
    outs = refs[11:14]
    zc_scr = refs[14]
    li = pl.program_id(0) % tiles_per_seq
    tl, c = main[0].shape
    row = lax.broadcasted_iota(jnp.int32, (tl, c), 0)
    keep_prev = jnp.where(li == 0, 0.0, 1.0)
    keep_next = jnp.where(li == tiles_per_seq - 1, 0.0, 1.0)
    last = BF16_SUBLANES - 1
    for g in range(3):
        z = main[g][...].astype(F32)
        z_prev = prev[g][last:last + 1, :].astype(F32) * keep_prev
        z_next = nxt[g][0:1, :].astype(F32) * keep_next
        before = jnp.where(row == 0, z_prev, pltpu.roll(z, 1, 0))
        after = jnp.where(row == tl - 1, z_next, pltpu.roll(z, tl - 1, 0))
        cols = slice(g * c, (g + 1) * c)
        zc = b_ref[:, cols] + before * w_ref[0:1, cols] + z * w_ref[1:2, cols] + after * w_ref[2:3, cols]
        for k in range(c // LANES):
            lanes = slice(k * LANES, (k + 1) * LANES)
            zc_scr[k] = zc[:, lanes]
            for p in range(PHASES):
                outs[g][p, :, lanes] = zc_scr[k, pl.ds(p, tl // PHASES, stride=PHASES), :].astype(outs[g].dtype)


def _shortconv(zin, conv_w, conv_b, *, batch, seq, col0):
    c = conv_w.shape[1] // 3
    tl = _tile(seq, SHORTCONV_ROWS)
    tiles_per_seq = seq // tl
    cb0 = col0 // c
    halo = tl // BF16_SUBLANES
    n_halo = zin.shape[0] // BF16_SUBLANES

    def main_spec(g):
        return pl.BlockSpec((tl, c), lambda i: (i, cb0 + g))

    def prev_spec(g):
        return pl.BlockSpec((BF16_SUBLANES, c), lambda i: (jnp.maximum(i * halo - 1, 0), cb0 + g))

    def next_spec(g):
        return pl.BlockSpec((BF16_SUBLANES, c), lambda i: (jnp.minimum((i + 1) * halo, n_halo - 1), cb0 + g))

    out_spec = pl.BlockSpec((PHASES, tl // PHASES, c), lambda i: (0, i % tiles_per_seq, i // tiles_per_seq))
    return pl.pallas_call(
        functools.partial(_shortconv_body, tiles_per_seq=tiles_per_seq),
        grid=(batch * tiles_per_seq,),
        in_specs=[main_spec(g) for g in range(3)] + [prev_spec(g) for g in range(3)]
        + [next_spec(g) for g in range(3)]
        + [pl.BlockSpec((3, 3 * c), lambda i: (0, 0)), pl.BlockSpec((1, 3 * c), lambda i: (0, 0))],
        out_specs=[out_spec] * 3,
        out_shape=[jax.ShapeDtypeStruct((PHASES, seq // PHASES, batch * c), BF16)] * 3,
        scratch_shapes=[pltpu.VMEM((c // LANES, tl, LANES), F32)],
        compiler_params=_params("parallel"),
        name="shortconv",
    )(*([zin] * 9), conv_w, conv_b.reshape(1, 3 * c))


def _dft_table_body(o_ref, *, seq, fb, transposed):
    tr, tc = o_ref.shape
    r0 = pl.program_id(0) * tr
    c0 = pl.program_id(1) * tc
    rows = lax.broadcasted_iota(jnp.int32, (tr, tc), 0) + r0
    cols = lax.broadcasted_iota(jnp.int32, (tr, tc), 1) + c0
    spec_idx, pos = (cols, rows) if transposed else (rows, cols)
    log_fb = fb.bit_length() - 1
    blk = lax.shift_right_logical(spec_idx, log_fb + 1)
    p = spec_idx & (2 * fb - 1)
    imag = p >= fb
    f = lax.shift_left(blk, log_fb) + (p & (fb - 1))
    nyquist = imag & (f == 0)
    f_eff = jnp.where(nyquist, seq, f)
    shift = jnp.where(imag & (f != 0), seq // 2, 0)
    q = (f_eff * pos + shift) & (2 * seq - 1)
    o_ref[...] = jnp.cos(q.astype(F32) * (math.pi / seq)).astype(o_ref.dtype)


def _dft_table(seq, fb, transposed):
    shape = (seq, 2 * seq) if transposed else (2 * seq, seq)
    tr, tc = _tile(shape[0], TABLE_TILE[0]), _tile(shape[1], TABLE_TILE[1])
    return pl.pallas_call(
        functools.partial(_dft_table_body, seq=seq, fb=fb, transposed=transposed),
        grid=(shape[0] // tr, shape[1] // tc),
        out_specs=pl.BlockSpec((tr, tc), lambda i, j: (i, j)),
        out_shape=jax.ShapeDtypeStruct(shape, BF16),
        compiler_params=_params("parallel", "parallel"),
        name="dft_table_inv" if transposed else "dft_table_fwd",
    )()


def _filter_mlp_body(z_ref, w1_ref, b1_ref, w2_ref, b2_ref, w3_ref, b3_ref, fr_ref, o_ref):
    def hdot(a, b):
        return jnp.dot(a, b, precision=HIGHEST, preferred_element_type=F32)

    h = jnp.sin(fr_ref[0:1, :] * (hdot(z_ref[...], w1_ref[...]) + b1_ref[...]))
    h = jnp.sin(fr_ref[1:2, :] * (hdot(h, w2_ref[...]) + b2_ref[...]))
    o_ref[...] = jnp.sin(fr_ref[2:3, :] * (hdot(h, w3_ref[...]) + b3_ref[...]))


def _filter_mlp(zpos, f_w1, f_b1, f_w2, f_b2, f_w3, f_b3, f_freq):
    depth, pe, hid = f_w1.shape
    seq = zpos.shape[0]
    zpad = jnp.pad(zpos, ((0, 0), (0, hid - pe)))
    w1pad = jnp.pad(f_w1, ((0, 0), (0, hid - pe), (0, 0)))
    mat = pl.BlockSpec((None, hid, hid), lambda l: (l, 0, 0))
    vec = pl.BlockSpec((None, 1, hid), lambda l: (l, 0, 0))
    return pl.pallas_call(
        _filter_mlp_body,
        grid=(depth,),
        in_specs=[pl.BlockSpec((seq, hid), lambda l: (0, 0)), mat, vec, mat, vec, mat, vec,
                  pl.BlockSpec((None, 3, hid), lambda l: (l, 0, 0))],
        out_specs=pl.BlockSpec((None, seq, hid), lambda l: (l, 0, 0)),
        out_shape=jax.ShapeDtypeStruct((depth, seq, hid), F32),
        compiler_params=_params("parallel"),
        name="filter_mlp",
    )(zpad, w1pad, f_b1.reshape(depth, 1, hid), f_w2, f_b2.reshape(depth, 1, hid),
      f_w3, f_b3.reshape(depth, 1, hid), f_freq)


def _filter_taps_body(h_ref, wf_ref, wb_ref, df_ref, db_ref, o_ref, fw_scr, bw_scr):
    seq, tc = h_ref.shape[0], o_ref.shape[-1]
    dec = seq // PHASES
    h = h_ref[...]
    n = lax.broadcasted_iota(jnp.int32, (seq, tc), 0)
    t = n.astype(F32) * (1.0 / (seq - 1))

    def taps(w_ref, d_ref):
        k = _dot(h.astype(BF16), w_ref[...].astype(BF16))
        return k * jnp.exp(-t * jnp.exp(d_ref[...]))

    fwd = taps(wf_ref, df_ref)
    bwd = jnp.where(n == 0, 0.0, taps(wb_ref, db_ref))
    scale = lax.rsqrt(jnp.sum(fwd * fwd + bwd * bwd, axis=0, keepdims=True) + EPS)
    fwd, bwd = fwd * scale, bwd * scale
    head_row = lax.broadcasted_iota(jnp.int32, (TAP_PAD, tc), 0)
    head = jnp.zeros((TAP_PAD, tc), F32)
    for lag in range(1, PHASES):
        head = jnp.where(head_row == TAP_PAD - lag, bwd[lag:lag + 1, :], head)
    fw_scr[0:TAP_PAD, :] = head
    fw_scr[TAP_PAD:TAP_PAD + seq, :] = fwd
    bw_scr[0:TAP_PAD, :] = jnp.zeros((TAP_PAD, tc), F32)
    bw_scr[TAP_PAD:TAP_PAD + seq, :] = bwd
    bw_scr[TAP_PAD + seq:2 * TAP_PAD + seq, :] = jnp.zeros((TAP_PAD, tc), F32)
    first = lax.broadcasted_iota(jnp.int32, (dec, tc), 0) == 0
    for d in range(1 - PHASES, PHASES):
        pos = fw_scr[pl.ds(TAP_PAD + d, dec, stride=PHASES), :]
        neg = jnp.where(first, 0.0, bw_scr[pl.ds(TAP_PAD - d, dec, stride=PHASES), :])
        o_ref[d + PHASES - 1, 0] = (pos + neg).astype(o_ref.dtype)
        o_ref[d + PHASES - 1, 1] = (pos - neg).astype(o_ref.dtype)


def _filter_taps(h3, f_w_out, log_decay, c):
    depth, seq, hid = h3.shape
    order = f_w_out.shape[2] // (2 * c)
    tc = _tile(c, LANES)
    ncb = c // tc
    nd = 2 * PHASES - 1
    ld = log_decay.reshape(depth, 1, -1)

    def wspec(direction):
        return pl.BlockSpec((None, hid, tc), lambda l, o, j: (l, 0, (2 * o + direction) * ncb + j))

    def dspec(direction):
        return pl.BlockSpec((None, 1, tc), lambda l, o, j: (l, 0, (2 * o + direction) * ncb + j))

    return pl.pallas_call(
        _filter_taps_body,
        grid=(depth, order, ncb),
        in_specs=[pl.BlockSpec((None, seq, hid), lambda l, o, j: (l, 0, 0)),
                  wspec(0), wspec(1), dspec(0), dspec(1)],
        out_specs=pl.BlockSpec((None, None, nd, 2, seq // PHASES, tc), lambda l, o, j: (l, o, 0, 0, 0, j)),
        out_shape=jax.ShapeDtypeStruct((depth, order, nd, 2, seq // PHASES, c), BF16),
        scratch_shapes=[pltpu.VMEM((seq + TAP_PAD, tc), F32), pltpu.VMEM((seq + 2 * TAP_PAD, tc), F32)],
        compiler_params=_params("parallel", "parallel", "parallel"),
        name="filter_taps",
    )(h3, f_w_out, f_w_out, ld, ld)


def _spectrum_body(t_ref, sa_ref, p_ref, *, fb, inv_len):
    r = pl.program_id(3)
    top = _dot(t_ref[0:fb, :], sa_ref[0]) * inv_len
    bot = _dot(t_ref[fb:2 * fb, :], sa_ref[1]) * inv_len
    p_ref[0:fb, :] = top.astype(p_ref.dtype)
    p_ref[fb:2 * fb, :] = bot.astype(p_ref.dtype)

    @pl.when(r == 0)
    def _():
        head = slice(0, BF16_SUBLANES)
        nyq = _dot(t_ref[fb:fb + BF16_SUBLANES, :], sa_ref[0]) * (0.5 * inv_len)
        row0 = lax.broadcasted_iota(jnp.int32, nyq.shape, 0) == 0
        p_ref[head, :] = jnp.where(row0, top[head, :] * 0.5, top[head, :]).astype(p_ref.dtype)
        p_ref[fb:fb + BF16_SUBLANES, :] = jnp.where(row0, nyq, bot[head, :]).astype(p_ref.dtype)


def _spectrum(table_fwd, taps, fb):
    depth, order, nd, _, dec, c = taps.shape
    tn = _tile(c, 2 * MXU_TILE)
    return pl.pallas_call(
        functools.partial(_spectrum_body, fb=fb, inv_len=1.0 / dec),
        grid=(depth, order * nd, c // tn, dec // fb),
        in_specs=[pl.BlockSpec((2 * fb, dec), lambda l, od, j, r: (r, 0)),
                  pl.BlockSpec((None, None, None, 2, dec, tn), lambda l, od, j, r: (l, od // nd, od % nd, 0, 0, j))],
        out_specs=pl.BlockSpec((None, None, None, 2 * fb, tn), lambda l, od, j, r: (l, od // nd, od % nd, r, j)),
        out_shape=jax.ShapeDtypeStruct((depth, order, nd, 2 * dec, c), BF16),
        compiler_params=_params("parallel", "parallel", "parallel", "parallel"),
        name="spectrum",
    )(table_fwd, taps)


def _poly_fwd_body(*refs, fb):
    nd = 2 * PHASES - 1
    t_ref = refs[0]
    x_refs = refs[1:1 + PHASES]
    g_refs = refs[1 + PHASES:1 + PHASES + nd]
    o_ref = refs[1 + PHASES + nd]
    i = pl.program_id(0)
    tm = t_ref.shape[0]
    for block0 in range(0, tm, 2 * fb):
        for chunk in _row_chunks(fb, SPECTRAL_ROWS):
            re = slice(block0 + chunk.start, block0 + chunk.stop)
            im = slice(re.start + fb, re.stop + fb)
            ar = [_dot(t_ref[re, :], x[...]).astype(BF16) for x in x_refs]
            ai = [_dot(t_ref[im, :], x[...]).astype(BF16) for x in x_refs]
            gr = [g[re, :] for g in g_refs]
            gi = [g[im, :] for g in g_refs]
            cross, diag = gi, gr
            if re.start == 0:
                row0 = (lax.broadcasted_iota(jnp.int32, gr[0].shape, 0) == 0) & (i == 0)
                cross = [jnp.where(row0, jnp.zeros_like(v), v) for v in gi]
                diag = [jnp.where(row0, vi, vr) for vr, vi in zip(gr, gi)]
            for m in range(PHASES):
                br = bi = None
                for p in range(PHASES):
                    d = m - p + PHASES - 1
                    tr = gr[d] * ar[p] - cross[d] * ai[p]
                    ti = cross[d] * ar[p] + diag[d] * ai[p]
                    br = tr if br is None else br + tr
                    bi = ti if bi is None else bi + ti
                o_ref[m, re, :] = br.astype(o_ref.dtype)
                o_ref[m, im, :] = bi.astype(o_ref.dtype)


def _poly_fwd(table_fwd, spectra, u, *, fb, c, layer, order_idx):
    _, dec, n = u.shape
    tm = _tile(2 * dec, MXU_TILE)
    tn = _tile(c, SPECTRAL_COLS)
    ncol, ncb, nd = n // tn, c // tn, 2 * PHASES - 1

    def phase_spec(p):
        return pl.BlockSpec((None, dec, tn), lambda i, j: (p, 0, j))

    def g_spec(d):
        return pl.BlockSpec((None, None, None, tm, tn), lambda i, j: (layer, order_idx, d, i, j % ncb))

    return pl.pallas_call(
        functools.partial(_poly_fwd_body, fb=fb),
        grid=(2 * dec // tm, ncol),
        in_specs=[pl.BlockSpec((tm, dec), lambda i, j: (i, 0))]
        + [phase_spec(p) for p in range(PHASES)] + [g_spec(d) for d in range(nd)],
        out_specs=pl.BlockSpec((PHASES, tm, tn), lambda i, j: (0, i, j)),
        out_shape=jax.ShapeDtypeStruct((PHASES, 2 * dec, n), BF16),
        compiler_params=_params("parallel", "parallel"),
        name="poly_fwd",
    )(table_fwd, *([u] * PHASES), *([spectra] * nd))


def _poly_inv_norm_body(t_ref, z_ref, u_ref, gate_ref, d_ref, g_ref, o_ref, rows_scr):
    p = pl.program_id(2)
    tm = t_ref.shape[0]
    y = gate_ref[...].astype(F32) * (_dot(t_ref[...], z_ref[...]) + d_ref[...] * u_ref[...].astype(F32))
    yn = _rms(y) * g_ref[...]
    slabs = [slice(k * LANES, (k + 1) * LANES) for k in range(yn.shape[1] // LANES)]
    for q in range(PHASES):
        @pl.when(p == q)
        def _(q=q):
            for k, lanes in enumerate(slabs):
                rows_scr[k, pl.ds(q, tm, stride=PHASES), :] = yn[:, lanes]

    @pl.when(p == PHASES - 1)
    def _():
        for k, lanes in enumerate(slabs):
            o_ref[:, lanes] = rows_scr[k].astype(o_ref.dtype)


def _poly_inv_norm(table_inv, z, u, gate, d, g, *, batch, order_idx):
    dec = table_inv.shape[0]
    c = g.shape[-1]
    tm = _tile(dec, INTERLEAVE_ROWS)
    tiles = dec // tm
    phase_tile = pl.BlockSpec((None, tm, c), lambda i, b, p: (p, i, b))
    return pl.pallas_call(
        _poly_inv_norm_body,
        grid=(tiles, batch, PHASES),
        in_specs=[pl.BlockSpec((tm, 2 * dec), lambda i, b, p: (i, 0)),
                  pl.BlockSpec((None, 2 * dec, c), lambda i, b, p: (p, 0, b)),
                  phase_tile, phase_tile,
                  pl.BlockSpec((None, 1, c), lambda i, b, p: (order_idx, 0, 0)),
                  pl.BlockSpec((1, c), lambda i, b, p: (0, 0))],
        out_specs=pl.BlockSpec((PHASES * tm, c), lambda i, b, p: (b * tiles + i, 0)),
        out_shape=jax.ShapeDtypeStruct((batch * PHASES * dec, c), BF16),
        scratch_shapes=[pltpu.VMEM((c // LANES, PHASES * tm, LANES), F32)],
        compiler_params=_params("parallel", "parallel", "arbitrary"),
        name="dft_inv_norm",
    )(table_inv, z, u, gate, d, g.reshape(1, c))


def _gate_epilogue(acc_of, i, j, extra_refs, out_refs):
    u_ref, gate_ref, d_ref = extra_refs
    (o_ref,) = out_refs
    for rows in _row_chunks(o_ref.shape[0], o_ref.shape[0] // 2):
        y = gate_ref[rows, :].astype(F32) * (acc_of(rows) + d_ref[...] * u_ref[rows, :].astype(F32))
        o_ref[rows, :] = y.astype(o_ref.dtype)


def _ffn_body(*refs, final):
    x_hbm, gn_ref, sc_ref, sh_ref, gate_ref, wu_ref, wd_ref, gnext_ref = refs[:8]
    if final:
        o_ref, h_scr, x_sem = refs[8:]
        acc_ref = o_ref
    else:
        scn_ref, shn_ref, acc_ref, o_ref, h_scr, x_sem = refs[8:]
    i, f = pl.program_id(0), pl.program_id(1)
    tm, d = acc_ref.shape
    row_chunks = _row_chunks(tm, ROW_CHUNK)

    @pl.when(f == 0)
    def _():
        load_x = pltpu.make_async_copy(x_hbm.at[pl.ds(pl.multiple_of(i * tm, tm), tm), :], acc_ref, x_sem)
        load_x.start()
        load_x.wait()
        for rows in row_chunks:
            h = _rms(acc_ref[rows, :]) * gn_ref[...] * (1.0 + sc_ref[...]) + sh_ref[...]
            h_scr[rows, :] = h.astype(h_scr.dtype)

    tn = _tile(d, FFN_COLS)
    for hidden in _row_chunks(wu_ref.shape[1], FFN_HIDDEN_CHUNK):
        hid = jnp.square(jnp.maximum(_dot(h_scr[...], wu_ref[:, hidden]), 0.0)).astype(BF16)
        for n in range(d // tn):
            cols = slice(n * tn, (n + 1) * tn)
            acc_ref[:, cols] += gate_ref[:, cols] * _dot(hid, wd_ref[hidden, cols])

    @pl.when(f == pl.num_programs(1) - 1)
    def _():
        for rows in row_chunks:
            xn = acc_ref[rows, :]
            if final:
                o_ref[rows, :] = _rms(xn) * gnext_ref[...]
            else:
                o_ref[rows, :] = (_rms(xn) * gnext_ref[...] * (1.0 + scn_ref[...]) + shn_ref[...]).astype(o_ref.dtype)


def _ffn(x, mod, g_norm, w_up, w_down, layer, g_next, mod_next, *, seq):
    t, d = x.shape
    dff = w_up.shape[2]
    tm, tf = _tile(seq, FFN_ROWS), _tile(dff, FFN_HIDDEN_STEP)
    final = mod_next is None

    def mod_spec(chunk):
        return pl.BlockSpec((None, 1, d), lambda i, f: ((i * tm) // seq, 0, chunk))

    vec = pl.BlockSpec((1, d), lambda i, f: (0, 0))
    tile = lambda: pl.BlockSpec((tm, d), lambda i, f: (i, 0), pipeline_mode=SINGLE)
    operands = [x, g_norm.reshape(1, d), mod, mod, mod, w_up, w_down, g_next.reshape(1, d)]
    in_specs = [pl.BlockSpec(memory_space=pl.ANY), vec, mod_spec(4), mod_spec(3), mod_spec(5),
                pl.BlockSpec((None, d, tf), lambda i, f: (layer, 0, f)),
                pl.BlockSpec((None, tf, d), lambda i, f: (layer, f, 0)), vec]
    if final:
        out_shape = [jax.ShapeDtypeStruct((t, d), F32)]
    else:
        operands += [mod_next, mod_next]
        in_specs += [mod_spec(1), mod_spec(0)]
        out_shape = [jax.ShapeDtypeStruct((t, d), F32), jax.ShapeDtypeStruct((t, d), BF16)]
    return pl.pallas_call(
        functools.partial(_ffn_body, final=final),
        grid=(t // tm, dff // tf),
        in_specs=in_specs,
        out_specs=[tile() for _ in out_shape],
        out_shape=out_shape,
        scratch_shapes=[pltpu.VMEM((tm, d), BF16), pltpu.SemaphoreType.DMA(())],
        compiler_params=_params("parallel", "arbitrary"),
        name="ffn_final" if final else "ffn",
    )(*operands)


def _hyena_pos_features(seq, pos_emb):
    t = jnp.linspace(0.0, 1.0, seq, dtype=F32)[:, None]
    bands = (pos_emb - 1) // 2
    w = 2.0 * math.pi * jnp.arange(seq, dtype=F32) / seq
    f = jnp.linspace(1e-4, bands - 1, bands, dtype=F32)
    ang = w[:, None] * f[None, :]
    return jnp.concatenate([t, jnp.cos(ang), -jnp.sin(ang)], axis=-1)


def _trunk(x, mod, p):
    batch, seq, d = x.shape
    depth = mod.shape[0]
    wa = p["ln_g_a"].shape[1]
    c = p["hyena_d"].shape[2]
    assert seq & (seq - 1) == 0 and seq >= 2 * PHASES * BF16_SUBLANES, "sequence length must be a power of two"
    assert wa == c and wa + c == p["w_out"].shape[1], "the two head groups must split the mixing width evenly"
    dec = seq // PHASES
    fb = min(FREQ_BLOCK, dec // 2)
    t = batch * seq
    ncols = batch * c
    x = x.reshape(t, d)
    tm = _tile(seq, MXU_TILE)
    tn = _tile(d, MXU_TILE)
    ndb = d // tn
    tile = lambda i, j: (i, j)


    table_fwd = _dft_table(dec, fb, False)
    table_inv = _dft_table(dec, fb, True)
    zpos = _hyena_pos_features(seq, p["f_w1"].shape[1])
    h3 = _filter_mlp(zpos, p["f_w1"], p["f_b1"], p["f_w2"], p["f_b2"], p["f_w3"], p["f_b3"], p["f_freq"])
    spectra = _spectrum(table_fwd, _filter_taps(h3, p["f_w_out"], p["log_decay"], c), fb)
    n_in = 2 * wa + 3 * c
    tn_in = _tile(n_in, MXU_TILE)
    tm_i = _tile(dec, MXU_TILE)
    tn_c = _tile(c, MXU_TILE * MXU_TILE // tm_i)
    ncb = c // tn_c
    tiles_per_phase = ncols // tn_c
    phase_tile = lambda i, j: (j // tiles_per_phase, i, j % tiles_per_phase)

    h = _norm_in(x, p["g_norm1"][0], mod[0], seq=seq)
    for l in range(depth):
        m = mod[l]
        zin = _matmul(h, p["w_in"], layer=l, tm=tm, tn=tn_in, tk=d, extras=[],
                      outs=[(jax.ShapeDtypeStruct((t, n_in), BF16), (tm, tn_in), tile)],
                      epilogue=_store_epilogue, name="w_in")[0]
        ya = _gmlp(zin, p["ln_g_a"][l], p["ln_b_a"][l], p["w_s"][l], p["b_s"][l], p["g_grp_a"][l])
        v, x1, x2 = _shortconv(zin, p["conv_w"][l], p["conv_b"][l], batch=batch, seq=seq, col0=2 * wa)
        dl = p["hyena_d"][l].reshape(-1, 1, c)
        z = _poly_fwd(table_fwd, spectra, v, fb=fb, c=c, layer=l, order_idx=0)
        y1 = _matmul(table_inv, z, col_groups=PHASES, tm=tm_i, tn=tn_c, tk=CONV_TILE_K,
                     extras=[(v, (None, tm_i, tn_c), phase_tile), (x1, (None, tm_i, tn_c), phase_tile),
                             (dl, (None, 1, tn_c), lambda i, j: (0, 0, j % ncb))],
                     outs=[(jax.ShapeDtypeStruct((PHASES, dec, ncols), BF16), (None, tm_i, tn_c), phase_tile)],
                     epilogue=_gate_epilogue, name="dft_inv")[0]
        z = _poly_fwd(table_fwd, spectra, y1, fb=fb, c=c, layer=l, order_idx=1)
        yb = _poly_inv_norm(table_inv, z, y1, x2, dl, p["g_grp_b"][l], batch=batch, order_idx=1)
        x = _matmul(ya, p["w_out"], a_tail=yb, layer=l, tm=tm, tn=tn, tk=wa,
                    extras=[(x, (tm, tn), tile),
                            (m, (None, 1, tn), lambda i, j: ((i * tm) // seq, 0, 2 * ndb + j))],
                    outs=[(jax.ShapeDtypeStruct((t, d), F32), (tm, tn), tile)],
                    epilogue=_residual_epilogue, name="w_out")[0]
        if l + 1 < depth:
            x, h = _ffn(x, m, p["g_norm2"][l], p["w_up"], p["w_down"], l, p["g_norm1"][l + 1], mod[l + 1], seq=seq)
        else:
            (out,) = _ffn(x, m, p["g_norm2"][l], p["w_up"], p["w_down"], l, p["g_final"], None, seq=seq)
    return out.reshape(batch, seq, d)


def kernel(x_prompt, x_sample, c_prompt, c_sample, w_ada, b_ada, g_norm1, g_norm2, w_in, ln_g_a, ln_b_a, w_s, b_s, conv_w, conv_b, f_w1, f_b1, f_w2, f_b2, f_w3, f_b3, f_freq, f_w_out, log_decay, hyena_d, g_grp_a, g_grp_b, w_out, w_up, w_down, g_final):
    bp, bs = c_prompt.shape[0], c_sample.shape[0]
    assert bp + bs <= MOD_ROWS
    depth, d, _ = w_ada.shape
    c_all = jnp.concatenate([c_prompt, c_sample, jnp.zeros((MOD_ROWS - bp - bs, d), F32)], axis=0)
    mod = _adaln(c_all, w_ada, b_ada).reshape(depth, MOD_ROWS, 1, N_MOD * d)
    p = dict(g_norm1=g_norm1, g_norm2=g_norm2, w_in=w_in.astype(BF16), ln_g_a=ln_g_a, ln_b_a=ln_b_a,
             w_s=w_s, b_s=b_s, conv_w=conv_w, conv_b=conv_b, f_w1=f_w1, f_b1=f_b1, f_w2=f_w2, f_b2=f_b2,
             f_w3=f_w3, f_b3=f_b3, f_freq=f_freq, f_w_out=f_w_out, log_decay=log_decay, hyena_d=hyena_d,
             g_grp_a=g_grp_a, g_grp_b=g_grp_b, w_out=w_out.astype(BF16), w_up=w_up.astype(BF16),
             w_down=w_down.astype(BF16), g_final=g_final)
    y_prompt = _trunk(x_prompt, mod[:, :bp], p)
    y_sample = _trunk(x_sample, mod[:, bp:bp + bs], p)
    return (y_prompt, y_sample)
```

```python
import functools
import math

import jax
import jax.numpy as jnp
from jax import lax
from jax.experimental import pallas as pl
from jax.experimental.pallas import tpu as pltpu

F32 = jnp.float32
BF16 = jnp.bfloat16
EPS = 1e-6
N_MOD = 6
MOD_ROWS = 16
BF16_SUBLANES = 16
LANES = 128
V7X_VMEM_LIMIT_BYTES = 56 * 1024 * 1024
MXU_TILE = 1024
FREQ_BLOCK = MXU_TILE // 2
CONV_TILE_K = 2048
PHASES = 4
TAP_PAD = 8
SPECTRAL_ROWS = 256
INTERLEAVE_ROWS = 256
ADALN_COLS = 512
SHORTCONV_ROWS = 256
SPECTRAL_COLS = 512
TABLE_TILE = (256, 2048)
FFN_ROWS = 1024
FFN_HIDDEN_STEP = 512
FFN_HIDDEN_CHUNK = 512
FFN_COLS = 512
ROW_CHUNK = 128
HIGHEST = lax.Precision.HIGHEST
SINGLE = pl.Buffered(1)

def _tile(dim, pref):
    t = min(dim, pref)
    while dim % t:
        t //= 2
    return t


def _params(*semantics):
    return pltpu.CompilerParams(dimension_semantics=semantics,
                                vmem_limit_bytes=V7X_VMEM_LIMIT_BYTES)


def _dot(a, b):
    return jnp.dot(a, b, preferred_element_type=F32)


def _row_chunks(rows, chunk):
    chunk = _tile(rows, chunk)
    return [slice(r, r + chunk) for r in range(0, rows, chunk)]


def _rms(x):
    return x * lax.rsqrt(jnp.mean(x * x, axis=-1, keepdims=True) + EPS)


def _matmul_body(*refs, nk, n_extra, n_out, has_tail, epilogue):
    a_ref = refs[0]
    tail_ref = refs[1] if has_tail else a_ref
    first_b = 2 if has_tail else 1
    b_ref = refs[first_b]
    extra_refs = refs[first_b + 1:first_b + 1 + n_extra]
    first_out = first_b + 1 + n_extra
    out_refs = refs[first_out:first_out + n_out]
    i, j, k = pl.program_id(0), pl.program_id(1), pl.program_id(2)
    if nk == 1:
        epilogue(lambda rows: _dot(a_ref[rows, :], b_ref[...]), i, j, extra_refs, out_refs)
        return
    acc_ref = refs[-1]

    @pl.when(k == 0)
    def _():
        acc_ref[...] = _dot(a_ref[...], b_ref[...])

    if nk > 2:
        @pl.when((k > 0) & (k < nk - 1))
        def _():
            acc_ref[...] += _dot(a_ref[...], b_ref[...])

    @pl.when(k == nk - 1)
    def _():
        epilogue(lambda rows: acc_ref[rows, :] + _dot(tail_ref[rows, :], b_ref[...]), i, j, extra_refs, out_refs)


def _matmul(a, b, *, tm, tn, tk, extras, outs, epilogue, name, layer=None, col_groups=None, a_tail=None):
    m, kdim = a.shape
    n = b.shape[-1] * (col_groups or 1)
    tm, tn, tk = _tile(m, tm), _tile(b.shape[-1], tn), _tile(kdim, tk)
    if a_tail is None:
        assert b.shape[-2] == kdim
        nk = kdim // tk
        in_specs = [pl.BlockSpec((tm, tk), lambda i, j, k: (i, k))]
    else:
        assert a_tail.shape == a.shape and b.shape[-2] == 2 * kdim and tk == kdim
        nk = 2
        in_specs = [pl.BlockSpec((tm, tk), lambda i, j, k: (i, 0))] * 2
    if col_groups is not None:
        per_group = b.shape[-1] // tn
        in_specs.append(pl.BlockSpec((None, tk, tn), lambda i, j, k: (j // per_group, k, j % per_group)))
    elif layer is None:
        in_specs.append(pl.BlockSpec((tk, tn), lambda i, j, k: (k, j)))
    else:
        in_specs.append(pl.BlockSpec((None, tk, tn), lambda i, j, k: (layer, k, j)))

    def lift(imap):
        return lambda i, j, k: imap(i, j)

    in_specs += [pl.BlockSpec(blk, lift(imap)) for _, blk, imap in extras]
    operands = ([a] if a_tail is None else [a, a_tail]) + [b] + [e for e, _, _ in extras]
    body = functools.partial(_matmul_body, nk=nk, n_extra=len(extras), n_out=len(outs),
                             has_tail=a_tail is not None, epilogue=epilogue)
    return pl.pallas_call(
        body,
        grid=(m // tm, n // tn, nk),
        in_specs=in_specs,
        out_specs=[pl.BlockSpec(blk, lift(imap)) for _, blk, imap in outs],
        out_shape=[o for o, _, _ in outs],
        scratch_shapes=[pltpu.VMEM((tm, tn), F32)] if nk > 1 else [],
        compiler_params=_params("parallel", "parallel", "arbitrary"),
        name=name,
    )(*operands)


def _store_epilogue(acc_of, i, j, extra_refs, out_refs):
    (o_ref,) = out_refs
    for rows in _row_chunks(o_ref.shape[0], o_ref.shape[0] // 2):
        o_ref[rows, :] = acc_of(rows).astype(o_ref.dtype)


def _residual_epilogue(acc_of, i, j, extra_refs, out_refs):
    x_ref, g_ref = extra_refs
    (o_ref,) = out_refs
    for rows in _row_chunks(o_ref.shape[0], o_ref.shape[0] // 2):
        o_ref[rows, :] = x_ref[rows, :] + g_ref[...] * acc_of(rows)


def _adaln_body(c_ref, w_ref, b_ref, o_ref):
    c = c_ref[...]
    cs = (c / (1.0 + jnp.exp(-c))).astype(BF16)
    o_ref[...] = _dot(cs, w_ref[...].astype(BF16)) + b_ref[...]


def _adaln(c_pad, w_ada, b_ada):
    depth, d, n = w_ada.shape
    tn = _tile(n, ADALN_COLS)
    return pl.pallas_call(
        _adaln_body,
        grid=(depth, n // tn),
        in_specs=[pl.BlockSpec((MOD_ROWS, d), lambda l, j: (0, 0)),
                  pl.BlockSpec((None, d, tn), lambda l, j: (l, 0, j)),
                  pl.BlockSpec((None, 1, tn), lambda l, j: (l, 0, j))],
        out_specs=pl.BlockSpec((None, MOD_ROWS, tn), lambda l, j: (l, 0, j)),
        out_shape=jax.ShapeDtypeStruct((depth, MOD_ROWS, n), F32),
        compiler_params=_params("parallel", "parallel"),
        name="adaln",
    )(c_pad, w_ada, b_ada.reshape(depth, 1, n))


def _norm_in_body(x_ref, g_ref, sc_ref, sh_ref, o_ref):
    o_ref[...] = (_rms(x_ref[...]) * g_ref[...] * (1.0 + sc_ref[...]) + sh_ref[...]).astype(o_ref.dtype)


def _norm_in(x, gain, mod, *, seq):
    t, d = x.shape
    tr = _tile(seq, ROW_CHUNK)
    tile = pl.BlockSpec((tr, d), lambda i: (i, 0))

    def mod_spec(chunk):
        return pl.BlockSpec((None, 1, d), lambda i: ((i * tr) // seq, 0, chunk))

    return pl.pallas_call(
        _norm_in_body,
        grid=(t // tr,),
        in_specs=[tile, pl.BlockSpec((1, d), lambda i: (0, 0)), mod_spec(1), mod_spec(0)],
        out_specs=tile,
        out_shape=jax.ShapeDtypeStruct((t, d), BF16),
        compiler_params=_params("parallel"),
        name="norm_in",
    )(x, gain.reshape(1, d), mod, mod)


def _gmlp_body(z_ref, lng_ref, lnb_ref, ws_ref, bs_ref, gg_ref, o_ref, gate_ref, *, chunk, heads, hd):
    wa = heads * hd
    z = z_ref[...].astype(F32)
    z = 0.5 * z * (1.0 + jnp.tanh(math.sqrt(2.0 / math.pi) * (z + 0.044715 * (z * z * z))))
    u = z[:, :wa]
    v = z[:, wa:]
    vc = v - jnp.mean(v, axis=-1, keepdims=True)
    vn = vc * lax.rsqrt(jnp.mean(vc * vc, axis=-1, keepdims=True) + EPS) * lng_ref[...] + lnb_ref[...]
    vb = vn.astype(BF16)
    for rows in _row_chunks(z.shape[0], chunk):
        for h in range(heads):
            cols = slice(h * hd, (h + 1) * hd)
            gate_ref[rows, cols] = _dot(ws_ref[h], vb[rows, cols]) + bs_ref[:, cols]
    o_ref[...] = (_rms(u * gate_ref[...]) * gg_ref[...]).astype(o_ref.dtype)


def _gmlp(zin, ln_g, ln_b, w_s, b_s, g_grp):
    t = zin.shape[0]
    heads, chunk, _ = w_s.shape
    wa = ln_g.shape[0]
    hd = wa // heads
    tr = _tile(t, 2 * chunk)
    bs_full = jnp.repeat(b_s.T, hd, axis=1)
    vec = pl.BlockSpec((1, wa), lambda i: (0, 0))
    return pl.pallas_call(
        functools.partial(_gmlp_body, chunk=chunk, heads=heads, hd=hd),
        grid=(t // tr,),
        in_specs=[pl.BlockSpec((tr, 2 * wa), lambda i: (i, 0)), vec, vec,
                  pl.BlockSpec((heads, chunk, chunk), lambda i: (0, 0, 0)),
                  pl.BlockSpec((chunk, wa), lambda i: (0, 0)), vec],
        out_specs=pl.BlockSpec((tr, wa), lambda i: (i, 0)),
        out_shape=jax.ShapeDtypeStruct((t, wa), BF16),
        scratch_shapes=[pltpu.VMEM((tr, wa), F32)],
        compiler_params=_params("parallel"),
        name="gmlp",
    )(zin, ln_g.reshape(1, wa), ln_b.reshape(1, wa), w_s.astype(BF16), bs_full, g_grp.reshape(1, wa))


def _shortconv_body(*refs, tiles_per_seq):
    main, prev, nxt = refs[0:3], refs[3:6], refs[6:9]
    w_ref, b_ref = refs[9], refs[10]
    outs = refs[11:14]
    zc_scr = refs[14]
    li = pl.program_id(0) % tiles_per_seq
    tl, c = main[0].shape
    row = lax.broadcasted_iota(jnp.int32, (tl, c), 0)
    keep_prev = jnp.where(li == 0, 0.0, 1.0)
    keep_next = jnp.where(li == tiles_per_seq - 1, 0.0, 1.0)
    last = BF16_SUBLANES - 1
    for g in range(3):
        z = main[g][...].astype(F32)
        z_prev = prev[g][last:last + 1, :].astype(F32) * keep_prev
        z_next = nxt[g][0:1, :].astype(F32) * keep_next
        before = jnp.where(row == 0, z_prev, pltpu.roll(z, 1, 0))
        after = jnp.where(row == tl - 1, z_next, pltpu.roll(z, tl - 1, 0))
        cols = slice(g * c, (g + 1) * c)
        zc = b_ref[:, cols] + before * w_ref[0:1, cols] + z * w_ref[1:2, cols] + after * w_ref[2:3, cols]
        for k in range(c // LANES):
            lanes = slice(k * LANES, (k + 1) * LANES)
            zc_scr[k] = zc[:, lanes]
            for p in range(PHASES):
                outs[g][p, :, lanes] = zc_scr[k, pl.ds(p, tl // PHASES, stride=PHASES), :].astype(outs[g].dtype)


def _shortconv(zin, conv_w, conv_b, *, batch, seq, col0):
    c = conv_w.shape[1] // 3
    tl = _tile(seq, SHORTCONV_ROWS)
    tiles_per_seq = seq // tl
    cb0 = col0 // c
    halo = tl // BF16_SUBLANES
    n_halo = zin.shape[0] // BF16_SUBLANES

    def main_spec(g):
        return pl.BlockSpec((tl, c), lambda i: (i, cb0 + g))

    def prev_spec(g):
        return pl.BlockSpec((BF16_SUBLANES, c), lambda i: (jnp.maximum(i * halo - 1, 0), cb0 + g))

    def next_spec(g):
        return pl.BlockSpec((BF16_SUBLANES, c), lambda i: (jnp.minimum((i + 1) * halo, n_halo - 1), cb0 + g))

    out_spec = pl.BlockSpec((PHASES, tl // PHASES, c), lambda i: (0, i % tiles_per_seq, i // tiles_per_seq))
    return pl.pallas_call(
        functools.partial(_shortconv_body, tiles_per_seq=tiles_per_seq),
        grid=(batch * tiles_per_seq,),
        in_specs=[main_spec(g) for g in range(3)] + [prev_spec(g) for g in range(3)]
        + [next_spec(g) for g in range(3)]
        + [pl.BlockSpec((3, 3 * c), lambda i: (0, 0)), pl.BlockSpec((1, 3 * c), lambda i: (0, 0))],
        out_specs=[out_spec] * 3,
        out_shape=[jax.ShapeDtypeStruct((PHASES, seq // PHASES, batch * c), BF16)] * 3,
        scratch_shapes=[pltpu.VMEM((c // LANES, tl, LANES), F32)],
        compiler_params=_params("parallel"),
        name="shortconv",
    )(*([zin] * 9), conv_w, conv_b.reshape(1, 3 * c))


def _dft_table_body(o_ref, *, seq, fb, transposed):
    tr, tc = o_ref.shape
    r0 = pl.program_id(0) * tr
    c0 = pl.program_id(1) * tc
    rows = lax.broadcasted_iota(jnp.int32, (tr, tc), 0) + r0
    cols = lax.broadcasted_iota(jnp.int32, (tr, tc), 1) + c0
    spec_idx, pos = (cols, rows) if transposed else (rows, cols)
    log_fb = fb.bit_length() - 1
    blk = lax.shift_right_logical(spec_idx, log_fb + 1)
    p = spec_idx & (2 * fb - 1)
    imag = p >= fb
    f = lax.shift_left(blk, log_fb) + (p & (fb - 1))
    nyquist = imag & (f == 0)
    f_eff = jnp.where(nyquist, seq, f)
    shift = jnp.where(imag & (f != 0), seq // 2, 0)
    q = (f_eff * pos + shift) & (2 * seq - 1)
    o_ref[...] = jnp.cos(q.astype(F32) * (math.pi / seq)).astype(o_ref.dtype)


def _dft_table(seq, fb, transposed):
    shape = (seq, 2 * seq) if transposed else (2 * seq, seq)
    tr, tc = _tile(shape[0], TABLE_TILE[0]), _tile(shape[1], TABLE_TILE[1])
    return pl.pallas_call(
        functools.partial(_dft_table_body, seq=seq, fb=fb, transposed=transposed),
        grid=(shape[0] // tr, shape[1] // tc),
        out_specs=pl.BlockSpec((tr, tc), lambda i, j: (i, j)),
        out_shape=jax.ShapeDtypeStruct(shape, BF16),
        compiler_params=_params("parallel", "parallel"),
        name="dft_table_inv" if transposed else "dft_table_fwd",
    )()


def _filter_mlp_body(z_ref, w1_ref, b1_ref, w2_ref, b2_ref, w3_ref, b3_ref, fr_ref, o_ref):
    def hdot(a, b):
        return jnp.dot(a, b, precision=HIGHEST, preferred_element_type=F32)

    h = jnp.sin(fr_ref[0:1, :] * (hdot(z_ref[...], w1_ref[...]) + b1_ref[...]))
    h = jnp.sin(fr_ref[1:2, :] * (hdot(h, w2_ref[...]) + b2_ref[...]))
    o_ref[...] = jnp.sin(fr_ref[2:3, :] * (hdot(h, w3_ref[...]) + b3_ref[...]))


def _filter_mlp(zpos, f_w1, f_b1, f_w2, f_b2, f_w3, f_b3, f_freq):
    depth, pe, hid = f_w1.shape
    seq = zpos.shape[0]
    zpad = jnp.pad(zpos, ((0, 0), (0, hid - pe)))
    w1pad = jnp.pad(f_w1, ((0, 0), (0, hid - pe), (0, 0)))
    mat = pl.BlockSpec((None, hid, hid), lambda l: (l, 0, 0))
    vec = pl.BlockSpec((None, 1, hid), lambda l: (l, 0, 0))
    return pl.pallas_call(
        _filter_mlp_body,
        grid=(depth,),
        in_specs=[pl.BlockSpec((seq, hid), lambda l: (0, 0)), mat, vec, mat, vec, mat, vec,
                  pl.BlockSpec((None, 3, hid), lambda l: (l, 0, 0))],
        out_specs=pl.BlockSpec((None, seq, hid), lambda l: (l, 0, 0)),
        out_shape=jax.ShapeDtypeStruct((depth, seq, hid), F32),
        compiler_params=_params("parallel"),
        name="filter_mlp",
    )(zpad, w1pad, f_b1.reshape(depth, 1, hid), f_w2, f_b2.reshape(depth, 1, hid),
      f_w3, f_b3.reshape(depth, 1, hid), f_freq)


def _filter_taps_body(h_ref, wf_ref, wb_ref, df_ref, db_ref, o_ref, fw_scr, bw_scr):
    seq, tc = h_ref.shape[0], o_ref.shape[-1]
    dec = seq // PHASES
    h = h_ref[...]
    n = lax.broadcasted_iota(jnp.int32, (seq, tc), 0)
    t = n.astype(F32) * (1.0 / (seq - 1))

    def taps(w_ref, d_ref):
        k = _dot(h.astype(BF16), w_ref[...].astype(BF16))
        return k * jnp.exp(-t * jnp.exp(d_ref[...]))

    fwd = taps(wf_ref, df_ref)
    bwd = jnp.where(n == 0, 0.0, taps(wb_ref, db_ref))
    scale = lax.rsqrt(jnp.sum(fwd * fwd + bwd * bwd, axis=0, keepdims=True) + EPS)
    fwd, bwd = fwd * scale, bwd * scale
    head_row = lax.broadcasted_iota(jnp.int32, (TAP_PAD, tc), 0)
    head = jnp.zeros((TAP_PAD, tc), F32)
    for lag in range(1, PHASES):
        head = jnp.where(head_row == TAP_PAD - lag, bwd[lag:lag + 1, :], head)
    fw_scr[0:TAP_PAD, :] = head
    fw_scr[TAP_PAD:TAP_PAD + seq, :] = fwd
    bw_scr[0:TAP_PAD, :] = jnp.zeros((TAP_PAD, tc), F32)
    bw_scr[TAP_PAD:TAP_PAD + seq, :] = bwd
    bw_scr[TAP_PAD + seq:2 * TAP_PAD + seq, :] = jnp.zeros((TAP_PAD, tc), F32)
    first = lax.broadcasted_iota(jnp.int32, (dec, tc), 0) == 0
    for d in range(1 - PHASES, PHASES):
        pos = fw_scr[pl.ds(TAP_PAD + d, dec, stride=PHASES), :]
        neg = jnp.where(first, 0.0, bw_scr[pl.ds(TAP_PAD - d, dec, stride=PHASES), :])
        o_ref[d + PHASES - 1, 0] = (pos + neg).astype(o_ref.dtype)
        o_ref[d + PHASES - 1, 1] = (pos - neg).astype(o_ref.dtype)


def _filter_taps(h3, f_w_out, log_decay, c):
    depth, seq, hid = h3.shape
    order = f_w_out.shape[2] // (2 * c)
    tc = _tile(c, LANES)
    ncb = c // tc
    nd = 2 * PHASES - 1
    ld = log_decay.reshape(depth, 1, -1)

    def wspec(direction):
        return pl.BlockSpec((None, hid, tc), lambda l, o, j: (l, 0, (2 * o + direction) * ncb + j))

    def dspec(direction):
        return pl.BlockSpec((None, 1, tc), lambda l, o, j: (l, 0, (2 * o + direction) * ncb + j))

    return pl.pallas_call(
        _filter_taps_body,
        grid=(depth, order, ncb),
        in_specs=[pl.BlockSpec((None, seq, hid), lambda l, o, j: (l, 0, 0)),
                  wspec(0), wspec(1), dspec(0), dspec(1)],
        out_specs=pl.BlockSpec((None, None, nd, 2, seq // PHASES, tc), lambda l, o, j: (l, o, 0, 0, 0, j)),
        out_shape=jax.ShapeDtypeStruct((depth, order, nd, 2, seq // PHASES, c), BF16),
        scratch_shapes=[pltpu.VMEM((seq + TAP_PAD, tc), F32), pltpu.VMEM((seq + 2 * TAP_PAD, tc), F32)],
        compiler_params=_params("parallel", "parallel", "parallel"),
        name="filter_taps",
    )(h3, f_w_out, f_w_out, ld, ld)


def _spectrum_body(t_ref, sa_ref, p_ref, *, fb, inv_len):
    r = pl.program_id(3)
    top = _dot(t_ref[0:fb, :], sa_ref[0]) * inv_len
    bot = _dot(t_ref[fb:2 * fb, :], sa_ref[1]) * inv_len
    p_ref[0:fb, :] = top.astype(p_ref.dtype)
    p_ref[fb:2 * fb, :] = bot.astype(p_ref.dtype)

    @pl.when(r == 0)
    def _():
        head = slice(0, BF16_SUBLANES)
        nyq = _dot(t_ref[fb:fb + BF16_SUBLANES, :], sa_ref[0]) * (0.5 * inv_len)
        row0 = lax.broadcasted_iota(jnp.int32, nyq.shape, 0) == 0
        p_ref[head, :] = jnp.where(row0, top[head, :] * 0.5, top[head, :]).astype(p_ref.dtype)
        p_ref[fb:fb + BF16_SUBLANES, :] = jnp.where(row0, nyq, bot[head, :]).astype(p_ref.dtype)


def _spectrum(table_fwd, taps, fb):
    depth, order, nd, _, dec, c = taps.shape
    tn = _tile(c, 2 * MXU_TILE)
    return pl.pallas_call(
        functools.partial(_spectrum_body, fb=fb, inv_len=1.0 / dec),
        grid=(depth, order * nd, c // tn, dec // fb),
        in_specs=[pl.BlockSpec((2 * fb, dec), lambda l, od, j, r: (r, 0)),
                  pl.BlockSpec((None, None, None, 2, dec, tn), lambda l, od, j, r: (l, od // nd, od % nd, 0, 0, j))],
        out_specs=pl.BlockSpec((None, None, None, 2 * fb, tn), lambda l, od, j, r: (l, od // nd, od % nd, r, j)),
        out_shape=jax.ShapeDtypeStruct((depth, order, nd, 2 * dec, c), BF16),
        compiler_params=_params("parallel", "parallel", "parallel", "parallel"),
        name="spectrum",
    )(table_fwd, taps)


def _poly_fwd_body(*refs, fb):
    nd = 2 * PHASES - 1
    t_ref = refs[0]
    x_refs = refs[1:1 + PHASES]
    g_refs = refs[1 + PHASES:1 + PHASES + nd]
    o_ref = refs[1 + PHASES + nd]
    i = pl.program_id(0)
    tm = t_ref.shape[0]
    for block0 in range(0, tm, 2 * fb):
        for chunk in _row_chunks(fb, SPECTRAL_ROWS):
            re = slice(block0 + chunk.start, block0 + chunk.stop)
            im = slice(re.start + fb, re.stop + fb)
            ar = [_dot(t_ref[re, :], x[...]).astype(BF16) for x in x_refs]
            ai = [_dot(t_ref[im, :], x[...]).astype(BF16) for x in x_refs]
            gr = [g[re, :] for g in g_refs]
            gi = [g[im, :] for g in g_refs]
            cross, diag = gi, gr
            if re.start == 0:
                row0 = (lax.broadcasted_iota(jnp.int32, gr[0].shape, 0) == 0) & (i == 0)
                cross = [jnp.where(row0, jnp.zeros_like(v), v) for v in gi]
                diag = [jnp.where(row0, vi, vr) for vr, vi in zip(gr, gi)]
            for m in range(PHASES):
                br = bi = None
                for p in range(PHASES):
                    d = m - p + PHASES - 1
                    tr = gr[d] * ar[p] - cross[d] * ai[p]
                    ti = cross[d] * ar[p] + diag[d] * ai[p]
                    br = tr if br is None else br + tr
                    bi = ti if bi is None else bi + ti
                o_ref[m, re, :] = br.astype(o_ref.dtype)
                o_ref[m, im, :] = bi.astype(o_ref.dtype)


def _poly_fwd(table_fwd, spectra, u, *, fb, c, layer, order_idx):
    _, dec, n = u.shape
    tm = _tile(2 * dec, MXU_TILE)
    tn = _tile(c, SPECTRAL_COLS)
    ncol, ncb, nd = n // tn, c // tn, 2 * PHASES - 1

    def phase_spec(p):
        return pl.BlockSpec((None, dec, tn), lambda i, j: (p, 0, j))

    def g_spec(d):
        return pl.BlockSpec((None, None, None, tm, tn), lambda i, j: (layer, order_idx, d, i, j % ncb))

    return pl.pallas_call(
        functools.partial(_poly_fwd_body, fb=fb),
        grid=(2 * dec // tm, ncol),
        in_specs=[pl.BlockSpec((tm, dec), lambda i, j: (i, 0))]
        + [phase_spec(p) for p in range(PHASES)] + [g_spec(d) for d in range(nd)],
        out_specs=pl.BlockSpec((PHASES, tm, tn), lambda i, j: (0, i, j)),
        out_shape=jax.ShapeDtypeStruct((PHASES, 2 * dec, n), BF16),
        compiler_params=_params("parallel", "parallel"),
        name="poly_fwd",
    )(table_fwd, *([u] * PHASES), *([spectra] * nd))


def _poly_inv_norm_body(t_ref, z_ref, u_ref, gate_ref, d_ref, g_ref, o_ref, rows_scr):
    p = pl.program_id(2)
    tm = t_ref.shape[0]
    y = gate_ref[...].astype(F32) * (_dot(t_ref[...], z_ref[...]) + d_ref[...] * u_ref[...].astype(F32))
    yn = _rms(y) * g_ref[...]
    slabs = [slice(k * LANES, (k + 1) * LANES) for k in range(yn.shape[1] // LANES)]
    for q in range(PHASES):
        @pl.when(p == q)
        def _(q=q):
            for k, lanes in enumerate(slabs):
                rows_scr[k, pl.ds(q, tm, stride=PHASES), :] = yn[:, lanes]

    @pl.when(p == PHASES - 1)
    def _():
        for k, lanes in enumerate(slabs):
            o_ref[:, lanes] = rows_scr[k].astype(o_ref.dtype)


def _poly_inv_norm(table_inv, z, u, gate, d, g, *, batch, order_idx):
    dec = table_inv.shape[0]
    c = g.shape[-1]
    tm = _tile(dec, INTERLEAVE_ROWS)
    tiles = dec // tm
    phase_tile = pl.BlockSpec((None, tm, c), lambda i, b, p: (p, i, b))
    return pl.pallas_call(
        _poly_inv_norm_body,
        grid=(tiles, batch, PHASES),
        in_specs=[pl.BlockSpec((tm, 2 * dec), lambda i, b, p: (i, 0)),
                  pl.BlockSpec((None, 2 * dec, c), lambda i, b, p: (p, 0, b)),
                  phase_tile, phase_tile,
                  pl.BlockSpec((None, 1, c), lambda i, b, p: (order_idx, 0, 0)),
                  pl.BlockSpec((1, c), lambda i, b, p: (0, 0))],
        out_specs=pl.BlockSpec((PHASES * tm, c), lambda i, b, p: (b * tiles + i, 0)),
        out_shape=jax.ShapeDtypeStruct((batch * PHASES * dec, c), BF16),
        scratch_shapes=[pltpu.VMEM((c // LANES, PHASES * tm, LANES), F32)],
        compiler_params=_params("parallel", "parallel", "arbitrary"),
        name="dft_inv_norm",
    )(table_inv, z, u, gate, d, g.reshape(1, c))


def _gate_epilogue(acc_of, i, j, extra_refs, out_refs):
    u_ref, gate_ref, d_ref = extra_refs
    (o_ref,) = out_refs
    for rows in _row_chunks(o_ref.shape[0], o_ref.shape[0] // 2):
        y = gate_ref[rows, :].astype(F32) * (acc_of(rows) + d_ref[...] * u_ref[rows, :].astype(F32))
        o_ref[rows, :] = y.astype(o_ref.dtype)


def _ffn_body(*refs, final):
    x_hbm, gn_ref, sc_ref, sh_ref, gate_ref, wu_ref, wd_ref, gnext_ref = refs[:8]
    if final:
        o_ref, h_scr, x_sem = refs[8:]
        acc_ref = o_ref
    else:
        scn_ref, shn_ref, acc_ref, o_ref, h_scr, x_sem = refs[8:]
    i, f = pl.program_id(0), pl.program_id(1)
    tm, d = acc_ref.shape
    row_chunks = _row_chunks(tm, ROW_CHUNK)

    @pl.when(f == 0)
    def _():
        row0 = pl.multiple_of(i * tm, tm)
        loads = [pltpu.make_async_copy(x_hbm.at[pl.ds(row0 + rows.start, rows.stop - rows.start), :],
                                       acc_ref.at[rows, :], x_sem.at[c])
                 for c, rows in enumerate(row_chunks)]
        for load in loads:
            load.start()
        for load, rows in zip(loads, row_chunks):
            load.wait()
            h = _rms(acc_ref[rows, :]) * gn_ref[...] * (1.0 + sc_ref[...]) + sh_ref[...]
            h_scr[rows, :] = h.astype(h_scr.dtype)

    tn = _tile(d, FFN_COLS)
    for hidden in _row_chunks(wu_ref.shape[1], FFN_HIDDEN_CHUNK):
        hid = jnp.square(jnp.maximum(_dot(h_scr[...], wu_ref[:, hidden]), 0.0)).astype(BF16)
        for n in range(d // tn):
            cols = slice(n * tn, (n + 1) * tn)
            acc_ref[:, cols] += gate_ref[:, cols] * _dot(hid, wd_ref[hidden, cols])

    @pl.when(f == pl.num_programs(1) - 1)
    def _():
        for rows in row_chunks:
            xn = acc_ref[rows, :]
            if final:
                o_ref[rows, :] = _rms(xn) * gnext_ref[...]
            else:
                o_ref[rows, :] = (_rms(xn) * gnext_ref[...] * (1.0 + scn_ref[...]) + shn_ref[...]).astype(o_ref.dtype)


def _ffn(x, mod, g_norm, w_up, w_down, layer, g_next, mod_next, *, seq):
    t, d = x.shape
    dff = w_up.shape[2]
    tm, tf = _tile(seq, FFN_ROWS), _tile(dff, FFN_HIDDEN_STEP)
    final = mod_next is None

    def mod_spec(chunk):
        return pl.BlockSpec((None, 1, d), lambda i, f: ((i * tm) // seq, 0, chunk))

    vec = pl.BlockSpec((1, d), lambda i, f: (0, 0))
    tile = lambda: pl.BlockSpec((tm, d), lambda i, f: (i, 0), pipeline_mode=SINGLE)
    operands = [x, g_norm.reshape(1, d), mod, mod, mod, w_up, w_down, g_next.reshape(1, d)]
    in_specs = [pl.BlockSpec(memory_space=pl.ANY), vec, mod_spec(4), mod_spec(3), mod_spec(5),
                pl.BlockSpec((None, d, tf), lambda i, f: (layer, 0, f)),
                pl.BlockSpec((None, tf, d), lambda i, f: (layer, f, 0)), vec]
    if final:
        out_shape = [jax.ShapeDtypeStruct((t, d), F32)]
    else:
        operands += [mod_next, mod_next]
        in_specs += [mod_spec(1), mod_spec(0)]
        out_shape = [jax.ShapeDtypeStruct((t, d), F32), jax.ShapeDtypeStruct((t, d), BF16)]
    return pl.pallas_call(
        functools.partial(_ffn_body, final=final),
        grid=(t // tm, dff // tf),
        in_specs=in_specs,
        out_specs=[tile() for _ in out_shape],
        out_shape=out_shape,
        scratch_shapes=[pltpu.VMEM((tm, d), BF16), pltpu.SemaphoreType.DMA((len(_row_chunks(tm, ROW_CHUNK)),))],
        compiler_params=_params("parallel", "arbitrary"),
        name="ffn_final" if final else "ffn",
    )(*operands)


def _hyena_pos_features(seq, pos_emb):
    t = jnp.linspace(0.0, 1.0, seq, dtype=F32)[:, None]
    bands = (pos_emb - 1) // 2
    w = 2.0 * math.pi * jnp.arange(seq, dtype=F32) / seq
    f = jnp.linspace(1e-4, bands - 1, bands, dtype=F32)
    ang = w[:, None] * f[None, :]
    return jnp.concatenate([t, jnp.cos(ang), -jnp.sin(ang)], axis=-1)


def _trunk(x, mod, p):
    batch, seq, d = x.shape
    depth = mod.shape[0]
    wa = p["ln_g_a"].shape[1]
    c = p["hyena_d"].shape[2]
    assert seq & (seq - 1) == 0 and seq >= 2 * PHASES * BF16_SUBLANES, "sequence length must be a power of two"
    assert wa == c and wa + c == p["w_out"].shape[1], "the two head groups must split the mixing width evenly"
    dec = seq // PHASES
    fb = min(FREQ_BLOCK, dec // 2)
    t = batch * seq
    ncols = batch * c
    x = x.reshape(t, d)
    tm = _tile(seq, MXU_TILE)
    tn = _tile(d, MXU_TILE)
    ndb = d // tn
    tile = lambda i, j: (i, j)


    table_fwd = _dft_table(dec, fb, False)
    table_inv = _dft_table(dec, fb, True)
    zpos = _hyena_pos_features(seq, p["f_w1"].shape[1])
    h3 = _filter_mlp(zpos, p["f_w1"], p["f_b1"], p["f_w2"], p["f_b2"], p["f_w3"], p["f_b3"], p["f_freq"])
    spectra = _spectrum(table_fwd, _filter_taps(h3, p["f_w_out"], p["log_decay"], c), fb)
    n_in = 2 * wa + 3 * c
    tn_in = _tile(n_in, MXU_TILE)
    tm_i = _tile(dec, MXU_TILE)
    tn_c = _tile(c, MXU_TILE * MXU_TILE // tm_i)
    ncb = c // tn_c
    tiles_per_phase = ncols // tn_c
    phase_tile = lambda i, j: (j // tiles_per_phase, i, j % tiles_per_phase)

    h = _norm_in(x, p["g_norm1"][0], mod[0], seq=seq)
    for l in range(depth):
        m = mod[l]
        zin = _matmul(h, p["w_in"], layer=l, tm=tm, tn=tn_in, tk=d, extras=[],
                      outs=[(jax.ShapeDtypeStruct((t, n_in), BF16), (tm, tn_in), tile)],
                      epilogue=_store_epilogue, name="w_in")[0]
        ya = _gmlp(zin, p["ln_g_a"][l], p["ln_b_a"][l], p["w_s"][l], p["b_s"][l], p["g_grp_a"][l])
        v, x1, x2 = _shortconv(zin, p["conv_w"][l], p["conv_b"][l], batch=batch, seq=seq, col0=2 * wa)
        dl = p["hyena_d"][l].reshape(-1, 1, c)
        z = _poly_fwd(table_fwd, spectra, v, fb=fb, c=c, layer=l, order_idx=0)
        y1 = _matmul(table_inv, z, col_groups=PHASES, tm=tm_i, tn=tn_c, tk=CONV_TILE_K,
                     extras=[(v, (None, tm_i, tn_c), phase_tile), (x1, (None, tm_i, tn_c), phase_tile),
                             (dl, (None, 1, tn_c), lambda i, j: (0, 0, j % ncb))],
                     outs=[(jax.ShapeDtypeStruct((PHASES, dec, ncols), BF16), (None, tm_i, tn_c), phase_tile)],
                     epilogue=_gate_epilogue, name="dft_inv")[0]
        z = _poly_fwd(table_fwd, spectra, y1, fb=fb, c=c, layer=l, order_idx=1)
        yb = _poly_inv_norm(table_inv, z, y1, x2, dl, p["g_grp_b"][l], batch=batch, order_idx=1)
        x = _matmul(ya, p["w_out"], a_tail=yb, layer=l, tm=tm, tn=tn, tk=wa,
                    extras=[(x, (tm, tn), tile),
                            (m, (None, 1, tn), lambda i, j: ((i * tm) // seq, 0, 2 * ndb + j))],
                    outs=[(jax.ShapeDtypeStruct((t, d), F32), (tm, tn), tile)],
                    epilogue=_residual_epilogue, name="w_out")[0]
        if l + 1 < depth:
            x, h = _ffn(x, m, p["g_norm2"][l], p["w_up"], p["w_down"], l, p["g_norm1"][l + 1], mod[l + 1], seq=seq)
        else:
            (out,) = _ffn(x, m, p["g_norm2"][l], p["w_up"], p["w_down"], l, p["g_final"], None, seq=seq)
    return out.reshape(batch, seq, d)


def kernel(x_prompt, x_sample, c_prompt, c_sample, w_ada, b_ada, g_norm1, g_norm2, w_in, ln_g_a, ln_b_a, w_s, b_s, conv_w, conv_b, f_w1, f_b1, f_w2, f_b2, f_w3, f_b3, f_freq, f_w_out, log_decay, hyena_d, g_grp_a, g_grp_b, w_out, w_up, w_down, g_final):
    bp, bs = c_prompt.shape[0], c_sample.shape[0]
    assert bp + bs <= MOD_ROWS
    depth, d, _ = w_ada.shape
    c_all = jnp.concatenate([c_prompt, c_sample, jnp.zeros((MOD_ROWS - bp - bs, d), F32)], axis=0)
    mod = _adaln(c_all, w_ada, b_ada).reshape(depth, MOD_ROWS, 1, N_MOD * d)
    p = dict(g_norm1=g_norm1, g_norm2=g_norm2, w_in=w_in.astype(BF16), ln_g_a=ln_g_a, ln_b_a=ln_b_a,
             w_s=w_s, b_s=b_s, conv_w=conv_w, conv_b=conv_b, f_w1=f_w1, f_b1=f_b1, f_w2=f_w2, f_b2=f_b2,
             f_w3=f_w3, f_b3=f_b3, f_freq=f_freq, f_w_out=f_w_out, log_decay=log_decay, hyena_d=hyena_d,
             g_grp_a=g_grp_a, g_grp_b=g_grp_b, w_out=w_out.astype(BF16), w_up=w_up.astype(BF16),
             w_down=w_down.astype(BF16), g_final=g_final)
    y_prompt = _trunk(x_prompt, mod[:, :bp], p)
    y_sample = _trunk(x_sample, mod[:, bp:bp + bs], p)
    return (y_prompt, y_sample)
```
